```python
import math
import jax, jax.numpy as jnp
from jax import lax
import numpy as np

D_MODEL = 1024
BATCH = 32
SEQ = 2048
DEPTH = 4

N_MIXERS = 2
N_CONV_LAYERS = (DEPTH + 1) // 2
N_LRU_LAYERS = DEPTH // 2
SC_WIDTH = 3
LRU_WIDTH = 1280
LRU_HEADS = 10
LRU_BLOCK = LRU_WIDTH // LRU_HEADS
LRU_CONV_WIDTH = 4
LRU_C = 8.0
FFN_HIDDEN = 2816
FFN_CONV_WIDTH = 3
LN_EPS = 1e-5
DEEPNORM_ALPHA = (2.0 * DEPTH) ** 0.25
DEEPNORM_BETA = (8.0 * DEPTH) ** -0.25

kernel_name = "hybrid_shortconv_rglru_convffn_deepnorm"


def causal_dwconv(x, w, b):
    k_width = w.shape[0]
    s = x.shape[1]
    xp = jnp.pad(x, ((0, 0), (k_width - 1, 0), (0, 0)))
    y = xp[:, 0:s] * w[0] + b
    for k in range(1, k_width):
        y = y + xp[:, k:k + s] * w[k]
    return y


def layer_norm(x, g, b):
    xf = x.astype(jnp.float32)
    mu = jnp.mean(xf, axis=-1, keepdims=True)
    var = jnp.mean(jnp.square(xf - mu), axis=-1, keepdims=True)
    y = (xf - mu) * lax.rsqrt(var + LN_EPS)
    return y.astype(x.dtype) * g + b


def short_conv_mixer(x, w_in, conv_w, conv_b, w_out):
    h = jnp.einsum('bsd,de->bse', x, w_in)
    gate_b, gate_c, v = jnp.split(h, 3, axis=-1)
    u = causal_dwconv(gate_c * v, conv_w, conv_b)
    return jnp.einsum('bsd,de->bse', gate_b * u, w_out)


def _lru_combine(left, right):
    a_l, b_l = left
    a_r, b_r = right
    return a_l * a_r, a_r * b_l + b_r


def rglru_block(x, w_in, b_in, conv_w, conv_b, w_gate, b_gate, lam, w_out):
    bsz, s, _ = x.shape
    h = jnp.einsum('bsd,de->bse', x, w_in) + b_in
    g_branch, r_branch = jnp.split(h, 2, axis=-1)
    xr = causal_dwconv(r_branch, conv_w, conv_b)
    xh = xr.reshape(bsz, s, LRU_HEADS, LRU_BLOCK)
    gates = jnp.einsum('bshi,hio->bsho', xh, w_gate) + b_gate
    r_gate, i_gate = jnp.split(gates.astype(jnp.float32), 2, axis=-1)
    r_gate = jax.nn.sigmoid(r_gate).reshape(bsz, s, LRU_WIDTH)
    i_gate = jax.nn.sigmoid(i_gate).reshape(bsz, s, LRU_WIDTH)
    log_a = -LRU_C * r_gate * jax.nn.softplus(-lam.astype(jnp.float32))
    a = jnp.exp(log_a)
    mult = jnp.sqrt(-jnp.expm1(2.0 * log_a))
    b = mult * (i_gate * xr.astype(jnp.float32))
    _, hs = lax.associative_scan(_lru_combine, (a, b), axis=1)
    y = hs.astype(x.dtype) * jax.nn.gelu(g_branch, approximate=True)
    return jnp.einsum('bsr,rd->bsd', y, w_out)


def conv_ffn(x, w_up, conv_w, conv_b, w_down):
    h = jnp.einsum('bsd,df->bsf', x, w_up)
    h = causal_dwconv(h, conv_w, conv_b)
    g, v = jnp.split(h, 2, axis=-1)
    return jnp.einsum('bsf,fd->bsd', jax.nn.silu(g) * v, w_down)


def setup_inputs(seed: int = 0) -> dict:
    key = jax.random.key(seed)
    ks = jax.random.split(key, 24)
    d, r, f = D_MODEL, LRU_WIDTH, FFN_HIDDEN
    nA, nB, L = N_CONV_LAYERS, N_LRU_LAYERS, DEPTH
    nrm = jax.random.normal
    x = nrm(ks[0], (BATCH, SEQ, d), jnp.float32)
    sc_w_in = nrm(ks[1], (nA, d, 3 * d), jnp.float32) * d ** -0.5
    sc_conv_w = nrm(ks[2], (nA, SC_WIDTH, d), jnp.float32) * SC_WIDTH ** -0.5
    sc_conv_b = nrm(ks[3], (nA, d), jnp.float32) * 0.01
    sc_w_out = nrm(ks[4], (nA, d, d), jnp.float32) * d ** -0.5 * DEEPNORM_BETA
    lru_w_in = nrm(ks[5], (nB, d, 2 * r), jnp.float32) * d ** -0.5
    lru_b_in = nrm(ks[6], (nB, 2 * r), jnp.float32) * 0.01
    lru_conv_w = nrm(ks[7], (nB, LRU_CONV_WIDTH, r), jnp.float32) * LRU_CONV_WIDTH ** -0.5
    lru_conv_b = nrm(ks[8], (nB, r), jnp.float32) * 0.01
    lru_w_gate = nrm(ks[9], (nB, LRU_HEADS, LRU_BLOCK, 2 * LRU_BLOCK), jnp.float32) * LRU_BLOCK ** -0.5
    lru_b_gate = nrm(ks[10], (nB, LRU_HEADS, 2 * LRU_BLOCK), jnp.float32) * 0.01
    u = jax.random.uniform(ks[11], (nB, r), jnp.float32, 0.9, 0.999)
    p = u ** (1.0 / LRU_C)
    lru_lambda = jnp.log(p) - jnp.log1p(-p)
    lru_w_out = nrm(ks[12], (nB, r, d), jnp.float32) * r ** -0.5 * DEEPNORM_BETA
    ffn_w_up = nrm(ks[13], (L, d, 2 * f), jnp.float32) * d ** -0.5
    ffn_conv_w = nrm(ks[14], (L, FFN_CONV_WIDTH, 2 * f), jnp.float32) * FFN_CONV_WIDTH ** -0.5
    ffn_conv_b = nrm(ks[15], (L, 2 * f), jnp.float32) * 0.01
    ffn_w_down = nrm(ks[16], (L, f, d), jnp.float32) * f ** -0.5 * DEEPNORM_BETA
    ln_g = 1.0 + 0.02 * nrm(ks[17], (L, 2, d), jnp.float32)
    ln_b = 0.02 * nrm(ks[18], (L, 2, d), jnp.float32)
    return {"x": x,
            "sc_w_in": sc_w_in, "sc_conv_w": sc_conv_w, "sc_conv_b": sc_conv_b, "sc_w_out": sc_w_out,
            "lru_w_in": lru_w_in, "lru_b_in": lru_b_in, "lru_conv_w": lru_conv_w, "lru_conv_b": lru_conv_b,
            "lru_w_gate": lru_w_gate, "lru_b_gate": lru_b_gate, "lru_lambda": lru_lambda, "lru_w_out": lru_w_out,
            "ffn_w_up": ffn_w_up, "ffn_conv_w": ffn_conv_w, "ffn_conv_b": ffn_conv_b, "ffn_w_down": ffn_w_down,
            "ln_g": ln_g, "ln_b": ln_b}


def reference(x, sc_w_in, sc_conv_w, sc_conv_b, sc_w_out,
              lru_w_in, lru_b_in, lru_conv_w, lru_conv_b, lru_w_gate, lru_b_gate, lru_lambda, lru_w_out,
              ffn_w_up, ffn_conv_w, ffn_conv_b, ffn_w_down, ln_g, ln_b):
    for i in range(DEPTH):
        j = i // N_MIXERS
        if i % N_MIXERS == 0:
            y = short_conv_mixer(x, sc_w_in[j], sc_conv_w[j], sc_conv_b[j], sc_w_out[j])
        else:
            y = rglru_block(x, lru_w_in[j], lru_b_in[j], lru_conv_w[j], lru_conv_b[j],
                            lru_w_gate[j], lru_b_gate[j], lru_lambda[j], lru_w_out[j])
        x = layer_norm(DEEPNORM_ALPHA * x + y, ln_g[i, 0], ln_b[i, 0])
        y = conv_ffn(x, ffn_w_up[i], ffn_conv_w[i], ffn_conv_b[i], ffn_w_down[i])
        x = layer_norm(DEEPNORM_ALPHA * x + y, ln_g[i, 1], ln_b[i, 1])
    return x
```

```python
import functools
import math

import jax
import jax.numpy as jnp
from jax import lax
from jax.experimental import pallas as pl
from jax.experimental.pallas import tpu as pltpu

LN_EPS = 1e-5
LRU_C = 8.0
ROW_BLOCK = 64
VMEM_LIMIT_BYTES = 56 * 1024 * 1024
_BF16 = jnp.bfloat16
_F32 = jnp.float32


def _sigmoid(z):
    return 1.0 / (1.0 + jnp.exp(-z))


def _gelu_tanh(z):
    c = math.sqrt(2.0 / math.pi)
    return 0.5 * z * (1.0 + jnp.tanh(c * (z + 0.044715 * (z * z * z))))


def _layer_norm_rows(z, g, b):
    mu = jnp.mean(z, axis=-1, keepdims=True)
    zc = z - mu
    var = jnp.mean(zc * zc, axis=-1, keepdims=True)
    return zc * lax.rsqrt(var + LN_EPS) * g + b


def _residual_ln(x_ref, y_ref, g_ref, b_ref, o_ref, alpha, tm):
    g = g_ref[...]
    b = b_ref[...]
    for r in range(0, tm, ROW_BLOCK):
        rows = pl.ds(r, ROW_BLOCK)
        z = alpha * x_ref[rows, :] + y_ref[rows, :]
        o_ref[rows, :] = _layer_norm_rows(z, g, b)


def _causal_conv(buf_ref, r, cols, w, bias, hist, batch):
    k_width = w.shape[0]
    acc = bias
    for k in range(k_width):
        off = hist - (k_width - 1 - k) * batch
        acc = acc + buf_ref[pl.ds(r + off, ROW_BLOCK), cols] * w[k:k + 1, :]
    return acc


def _ffn_kernel(x_ref, wup_ref, cw_ref, cb_ref, wdn_ref, g_ref, b_ref, o_ref,
                hbuf, act, ybuf, *, alpha, batch, tm, f, fc):
    hist = (cw_ref.shape[0] - 1) * batch

    @pl.when(pl.program_id(0) == 0)
    def _():
        hbuf[0:hist, :] = jnp.zeros((hist, hbuf.shape[1]), _F32)

    xb = x_ref[...].astype(_BF16)
    for c in range(0, 2 * f, fc):
        hbuf[hist:hist + tm, c:c + fc] = jnp.dot(
            xb, wup_ref[:, c:c + fc], preferred_element_type=_F32)

    for c in range(0, f, fc):
        gcols = slice(c, c + fc)
        vcols = slice(f + c, f + c + fc)
        wg, bg = cw_ref[:, gcols], cb_ref[:, gcols]
        wv, bv = cw_ref[:, vcols], cb_ref[:, vcols]
        for r in range(0, tm, ROW_BLOCK):
            gate = _causal_conv(hbuf, r, gcols, wg, bg, hist, batch)
            val = _causal_conv(hbuf, r, vcols, wv, bv, hist, batch)
            act[pl.ds(r, ROW_BLOCK), gcols] = (gate * _sigmoid(gate) * val).astype(_BF16)

    hbuf[0:hist, :] = hbuf[tm:tm + hist, :]
    ybuf[...] = jnp.dot(act[...], wdn_ref[...], preferred_element_type=_F32)
    _residual_ln(x_ref, ybuf, g_ref, b_ref, o_ref, alpha, tm)


def _sconv_kernel(x_ref, win_ref, cw_ref, cb_ref, wout_ref, g_ref, b_ref, o_ref,
                  hbuf, cvbuf, zbuf, ybuf, *, alpha, batch, tm, d):
    hist = (cw_ref.shape[0] - 1) * batch

    @pl.when(pl.program_id(0) == 0)
    def _():
        cvbuf[0:hist, :] = jnp.zeros((hist, d), _F32)

    xb = x_ref[...].astype(_BF16)
    hbuf[...] = jnp.dot(xb, win_ref[...], preferred_element_type=_F32)

    for r in range(0, tm, ROW_BLOCK):
        rows = pl.ds(r, ROW_BLOCK)
        cvbuf[pl.ds(hist + r, ROW_BLOCK), :] = hbuf[rows, d:2 * d] * hbuf[rows, 2 * d:3 * d]
    w = cw_ref[...]
    bias = cb_ref[...]
    for r in range(0, tm, ROW_BLOCK):
        rows = pl.ds(r, ROW_BLOCK)
        u = _causal_conv(cvbuf, r, slice(0, d), w, bias, hist, batch)
        zbuf[rows, :] = (hbuf[rows, 0:d] * u).astype(_BF16)
    cvbuf[0:hist, :] = cvbuf[tm:tm + hist, :]

    ybuf[...] = jnp.dot(zbuf[...], wout_ref[...], preferred_element_type=_F32)
    _residual_ln(x_ref, ybuf, g_ref, b_ref, o_ref, alpha, tm)


def _rglru_kernel(x_ref, win_ref, bin_ref, cw_ref, cb_ref, wg_ref, bg_ref, lam_ref, wout_ref,
                  g_ref, b_ref, o_ref,
                  gbuf, rbuf, xrbuf, xrb16, abuf, bbuf, state, zbuf, ybuf,
                  *, alpha, batch, tm, r_width, heads):
    hist = (cw_ref.shape[0] - 1) * batch
    blk = r_width // heads

    @pl.when(pl.program_id(0) == 0)
    def _():
        rbuf[0:hist, :] = jnp.zeros((hist, r_width), _F32)
        state[...] = jnp.zeros(state.shape, _F32)

    xb = x_ref[...].astype(_BF16)
    gbuf[...] = (jnp.dot(xb, win_ref[:, 0:r_width], preferred_element_type=_F32)
                 + bin_ref[:, 0:r_width])
    rbuf[hist:hist + tm, :] = (
        jnp.dot(xb, win_ref[:, r_width:2 * r_width], preferred_element_type=_F32)
        + bin_ref[:, r_width:2 * r_width])

    w = cw_ref[...]
    bias = cb_ref[...]
    for r in range(0, tm, ROW_BLOCK):
        rows = pl.ds(r, ROW_BLOCK)
        xr = _causal_conv(rbuf, r, slice(0, r_width), w, bias, hist, batch)
        xrbuf[rows, :] = xr
        xrb16[rows, :] = xr.astype(_BF16)
    rbuf[0:hist, :] = rbuf[tm:tm + hist, :]

    neg_lam = -lam_ref[...]
    softplus = jnp.maximum(neg_lam, 0.0) + jnp.log1p(jnp.exp(-jnp.abs(neg_lam)))
    log_a_scale = -LRU_C * softplus

    for h in range(heads):
        cols = slice(h * blk, (h + 1) * blk)
        gates = jnp.dot(xrb16[:, cols], wg_ref[h], preferred_element_type=_F32) + bg_ref[h]
        r_gate = _sigmoid(gates[:, 0:blk])
        i_gate = _sigmoid(gates[:, blk:2 * blk])
        log_a = r_gate * log_a_scale[:, cols]
        a = jnp.exp(log_a)
        abuf[:, cols] = a
        mult = jnp.sqrt(-jnp.tanh(log_a) * (a * a + 1.0))
        bbuf[:, cols] = mult * (i_gate * xrbuf[:, cols])

    hstate = state[...]
    for t in range(tm // batch):
        rows = pl.ds(t * batch, batch)
        hstate = abuf[rows, :] * hstate + bbuf[rows, :]
        zbuf[rows, :] = (hstate * _gelu_tanh(gbuf[rows, :])).astype(_BF16)
    state[...] = hstate

    ybuf[...] = jnp.dot(zbuf[...], wout_ref[...], preferred_element_type=_F32)
    _residual_ln(x_ref, ybuf, g_ref, b_ref, o_ref, alpha, tm)


def _resident(shape):
    zeros = (0,) * len(shape)
    return pl.BlockSpec(shape, lambda i: zeros, pipeline_mode=pl.Buffered(1))


def _sublayer_call(body, name, x2, operands, scratch, tm):
    rows, d = x2.shape
    row_spec = pl.BlockSpec((tm, d), lambda i: (i, 0))
    return pl.pallas_call(
        body,
        out_shape=jax.ShapeDtypeStruct((rows, d), _F32),
        grid=(rows // tm,),
        in_specs=[row_spec] + [_resident(op.shape) for op in operands],
        out_specs=row_spec,
        scratch_shapes=scratch,
        compiler_params=pltpu.CompilerParams(
            dimension_semantics=("arbitrary",),
            vmem_limit_bytes=VMEM_LIMIT_BYTES),
        name=name,
    )(x2, *operands)


def _row(v):
    return v.reshape(1, -1)


def kernel(x, sc_w_in, sc_conv_w, sc_conv_b, sc_w_out, lru_w_in, lru_b_in, lru_conv_w, lru_conv_b,
           lru_w_gate, lru_b_gate, lru_lambda, lru_w_out, ffn_w_up, ffn_conv_w, ffn_conv_b,
           ffn_w_down, ln_g, ln_b):
    batch, seq, d = x.shape
    depth = ffn_w_up.shape[0]
    f = ffn_w_down.shape[1]
    r_width = lru_w_out.shape[1]
    heads = lru_w_gate.shape[1]
    alpha = (2.0 * depth) ** 0.25
    tm = 512
    assert tm % batch == 0 and (seq * batch) % tm == 0 and batch % 8 == 0

    x2 = jnp.transpose(x, (1, 0, 2)).reshape(seq * batch, d)
    vm = pltpu.VMEM
    for i in range(depth):
        j = i // 2
        if i % 2 == 0:
            hist = (sc_conv_w.shape[1] - 1) * batch
            body = functools.partial(_sconv_kernel, alpha=alpha, batch=batch, tm=tm, d=d)
            x2 = _sublayer_call(
                body, "sconv_mixer", x2,
                [sc_w_in[j].astype(_BF16), sc_conv_w[j], _row(sc_conv_b[j]),
                 sc_w_out[j].astype(_BF16), _row(ln_g[i, 0]), _row(ln_b[i, 0])],
                [vm((tm, 3 * d), _F32), vm((hist + tm, d), _F32), vm((tm, d), _BF16),
                 vm((tm, d), _F32)], tm)
        else:
            hist = (lru_conv_w.shape[1] - 1) * batch
            body = functools.partial(_rglru_kernel, alpha=alpha, batch=batch, tm=tm,
                                     r_width=r_width, heads=heads)
            x2 = _sublayer_call(
                body, "rglru_mixer", x2,
                [lru_w_in[j].astype(_BF16), _row(lru_b_in[j]), lru_conv_w[j], _row(lru_conv_b[j]),
                 lru_w_gate[j].astype(_BF16), lru_b_gate[j].reshape(heads, 1, -1),
                 _row(lru_lambda[j]), lru_w_out[j].astype(_BF16),
                 _row(ln_g[i, 0]), _row(ln_b[i, 0])],
                [vm((tm, r_width), _F32), vm((hist + tm, r_width), _F32),
                 vm((tm, r_width), _F32), vm((tm, r_width), _BF16),
                 vm((tm, r_width), _F32), vm((tm, r_width), _F32),
                 vm((batch, r_width), _F32), vm((tm, r_width), _BF16), vm((tm, d), _F32)], tm)
        hist = (ffn_conv_w.shape[1] - 1) * batch
        body = functools.partial(_ffn_kernel, alpha=alpha, batch=batch, tm=tm, f=f, fc=256)
        x2 = _sublayer_call(
            body, "conv_ffn", x2,
            [ffn_w_up[i].astype(_BF16), ffn_conv_w[i], _row(ffn_conv_b[i]),
             ffn_w_down[i].astype(_BF16), _row(ln_g[i, 1]), _row(ln_b[i, 1])],
            [vm((hist + tm, 2 * f), _F32), vm((tm, f), _BF16), vm((tm, d), _F32)], tm)
    return jnp.transpose(x2.reshape(seq, batch, d), (1, 0, 2))
```

```python
import functools
import math

import jax
import jax.numpy as jnp
from jax import lax
from jax.experimental import pallas as pl
from jax.experimental.pallas import tpu as pltpu

LN_EPS = 1e-5
LRU_C = 8.0
ROW_TILE = 512
ROW_BLOCK = 32
COL_BLOCK = 256
LANES = 128
FFN_LEAD_PAIRS = 2
SCONV_LEAD_PAIRS = 2
RGLRU_LEAD_G_CHUNKS = 3
VMEM_LIMIT_BYTES = 56 * 1024 * 1024
F32_MIN_NORMAL = 1.1754943508222875e-38
_BF16 = jnp.bfloat16
_F32 = jnp.float32


def _gelu_tanh(z):
    c = math.sqrt(2.0 / math.pi)
    half = 0.5 * z
    return half + half * jnp.tanh(z * (c + (c * 0.044715) * (z * z)))


def _residual_ln_store(x_ref, z_ref, g_ref, b_ref, o_ref, mu_ref, rs_ref, alpha, tm):
    d = z_ref.shape[1]
    lanes = mu_ref.shape[1]
    inv_d = 1.0 / d

    def lane_total(r, fn):
        rows = pl.ds(r, ROW_BLOCK)
        acc = fn(rows, 0)
        for c in range(lanes, d, lanes):
            acc = acc + fn(rows, c)
        return jnp.broadcast_to(jnp.sum(acc, axis=-1, keepdims=True), (ROW_BLOCK, lanes))

    def residual_sum(rows, c):
        cols = slice(c, c + lanes)
        z = alpha * x_ref[rows, cols] + z_ref[rows, cols]
        z_ref[rows, cols] = z
        return z

    for r in range(0, tm, ROW_BLOCK):
        mu_ref[pl.ds(r, ROW_BLOCK), :] = inv_d * lane_total(r, residual_sum)

    def centred_sq(rows, c):
        zc = z_ref[rows, c:c + lanes] - mu_ref[rows, :]
        return zc * zc

    for r in range(0, tm, ROW_BLOCK):
        rs_ref[pl.ds(r, ROW_BLOCK), :] = lax.rsqrt(inv_d * lane_total(r, centred_sq) + LN_EPS)

    for r in range(0, tm, ROW_BLOCK):
        rows = pl.ds(r, ROW_BLOCK)
        mu = mu_ref[rows, :]
        rs = rs_ref[rows, :]
        for c in range(0, d, lanes):
            cols = slice(c, c + lanes)
            o_ref[rows, cols] = (z_ref[rows, cols] - mu) * rs * g_ref[:, cols] + b_ref[:, cols]


def _causal_conv(buf_ref, r, cols, w, bias, hist, batch):
    k_width = w.shape[0]
    acc = bias
    for k in range(k_width):
        off = hist - (k_width - 1 - k) * batch
        acc = acc + buf_ref[pl.ds(r + off, ROW_BLOCK), cols] * w[k:k + 1, :]
    return acc


def _matmul(lhs_f32, rhs_bf16):
    return lax.dot_general(lhs_f32, rhs_bf16, (((1,), (0,)), ((), ())),
                           preferred_element_type=_F32)


def _project(x_ref, w_ref, cols):
    return _matmul(x_ref[...], w_ref[:, cols])


def _ffn_kernel(x_ref, xn_ref, wup_ref, cw_ref, cb_ref, wdn_ref, g_ref, b_ref, o_ref,
                hbuf, act, zbuf, stats, *, alpha, batch, tm, f):
    hist = (cw_ref.shape[0] - 1) * batch
    fc = COL_BLOCK
    n_pairs = f // fc
    cur = slice(hist, hist + tm)

    def up_pair(rows_ref, p):
        for c in (p * fc, f + p * fc):
            hbuf[cur, c:c + fc] = _project(rows_ref, wup_ref, slice(c, c + fc))

    @pl.when(pl.program_id(0) == 0)
    def _():
        hbuf[0:hist, :] = jnp.zeros((hist, hbuf.shape[1]), _F32)
        for p in range(FFN_LEAD_PAIRS):
            up_pair(x_ref, p)

    for p in range(FFN_LEAD_PAIRS, n_pairs):
        up_pair(x_ref, p)

    for p in range(n_pairs):
        gcols = slice(p * fc, (p + 1) * fc)
        vcols = slice(f + p * fc, f + (p + 1) * fc)
        wg, bg = cw_ref[:, gcols], cb_ref[:, gcols]
        wv, bv = cw_ref[:, vcols], cb_ref[:, vcols]
        for r in range(0, tm, ROW_BLOCK):
            gate = _causal_conv(hbuf, r, gcols, wg, bg, hist, batch)
            val = _causal_conv(hbuf, r, vcols, wv, bv, hist, batch)
            half = 0.5 * gate
            act[pl.ds(r, ROW_BLOCK), gcols] = (half + half * jnp.tanh(half)) * val
        for cols in (gcols, vcols):
            hbuf[0:hist, cols] = hbuf[tm:tm + hist, cols]

    zbuf[...] = _matmul(act[...], wdn_ref[...])

    _residual_ln_store(x_ref, zbuf, g_ref, b_ref, o_ref, stats.at[0], stats.at[1], alpha, tm)

    for p in range(FFN_LEAD_PAIRS):
        up_pair(xn_ref, p)


def _sconv_kernel(x_ref, xn_ref, win_ref, cw_ref, cb_ref, wout_ref, g_ref, b_ref, o_ref,
                  hbuf, cvbuf, gated, zbuf, stats, *, alpha, batch, tm, d):
    hist = (cw_ref.shape[0] - 1) * batch
    dc = COL_BLOCK
    n_chunks = d // dc

    def cv_pair(rows_ref, p):
        for c in (d + p * dc, 2 * d + p * dc):
            hbuf[:, c:c + dc] = _project(rows_ref, win_ref, slice(c, c + dc))

    @pl.when(pl.program_id(0) == 0)
    def _():
        cvbuf[0:hist, :] = jnp.zeros((hist, d), _F32)
        for p in range(SCONV_LEAD_PAIRS):
            cv_pair(x_ref, p)

    for p in range(SCONV_LEAD_PAIRS, n_chunks):
        cv_pair(x_ref, p)
    for p in range(n_chunks):
        hbuf[:, p * dc:(p + 1) * dc] = _project(x_ref, win_ref, slice(p * dc, (p + 1) * dc))

    for p in range(n_chunks):
        c = p * dc
        for r in range(0, tm, ROW_BLOCK):
            rows = pl.ds(r, ROW_BLOCK)
            cvbuf[pl.ds(hist + r, ROW_BLOCK), c:c + dc] = (
                hbuf[rows, d + c:d + c + dc] * hbuf[rows, 2 * d + c:2 * d + c + dc])
    for p in range(n_chunks):
        cols = slice(p * dc, (p + 1) * dc)
        w, bias = cw_ref[:, cols], cb_ref[:, cols]
        for r in range(0, tm, ROW_BLOCK):
            rows = pl.ds(r, ROW_BLOCK)
            u = _causal_conv(cvbuf, r, cols, w, bias, hist, batch)
            gated[rows, cols] = hbuf[rows, cols] * u
        cvbuf[0:hist, cols] = cvbuf[tm:tm + hist, cols]

    zbuf[...] = _matmul(gated[...], wout_ref[...])

    for p in range(SCONV_LEAD_PAIRS):
        cv_pair(xn_ref, p)

    _residual_ln_store(x_ref, zbuf, g_ref, b_ref, o_ref, stats.at[0], stats.at[1], alpha, tm)


def _rglru_kernel(x_ref, xn_ref, win_ref, bin_ref, cw_ref, cb_ref, wg_ref, bg_ref, lam_ref,
                  wout_ref, g_ref, b_ref, o_ref,
                  rbuf, gbr, xr, gates, state, gated, zbuf, stats,
                  *, alpha, batch, tm, r_width, heads):
    hist = (cw_ref.shape[0] - 1) * batch
    blk = r_width // heads
    ic = COL_BLOCK
    n_chunks = r_width // ic
    cur = slice(hist, hist + tm)

    w = cw_ref[...]
    w_sum = w[0:1, :]
    for k in range(1, w.shape[0]):
        w_sum = w_sum + w[k:k + 1, :]
    conv_bias = cb_ref[...] + bin_ref[:, r_width:2 * r_width] * w_sum

    def r_chunk(rows_ref, j):
        rbuf[cur, j * ic:(j + 1) * ic] = _project(
            rows_ref, win_ref, slice(r_width + j * ic, r_width + (j + 1) * ic))

    def g_chunk(rows_ref, j):
        gbr[:, j * ic:(j + 1) * ic] = _project(rows_ref, win_ref, slice(j * ic, (j + 1) * ic))

    def conv_chunk(j):
        cols = slice(j * ic, (j + 1) * ic)
        wc, bc = w[:, cols], conv_bias[:, cols]
        for r in range(0, tm, ROW_BLOCK):
            rows = pl.ds(r, ROW_BLOCK)
            xr[rows, cols] = _causal_conv(rbuf, r, cols, wc, bc, hist, batch)
        rbuf[0:hist, cols] = rbuf[tm:tm + hist, cols]

    @pl.when(pl.program_id(0) == 0)
    def _():
        rbuf[0:hist, :] = jnp.broadcast_to(-bin_ref[:, r_width:2 * r_width], (hist, r_width))
        state[...] = jnp.zeros(state.shape, _F32)
        for j in range(n_chunks):
            r_chunk(x_ref, j)
        for j in range(n_chunks):
            conv_chunk(j)
        for j in range(RGLRU_LEAD_G_CHUNKS):
            g_chunk(x_ref, j)

    neg_lam = -lam_ref[...]
    softplus = jnp.maximum(neg_lam, 0.0) + jnp.log1p(jnp.exp(-jnp.abs(neg_lam)))
    half_scale = (-0.5 * LRU_C) * softplus

    side_work = ([functools.partial(g_chunk, x_ref, j)
                  for j in range(RGLRU_LEAD_G_CHUNKS, n_chunks)]
                 + [functools.partial(r_chunk, xn_ref, j) for j in range(n_chunks)])
    for h in range(heads):
        cols = slice(h * blk, (h + 1) * blk)
        gts = gates.at[h % 2]

        gts[...] = _matmul(xr[:, cols], wg_ref[h])
        if h < len(side_work):
            side_work[h]()

        hs = half_scale[:, cols]
        half_bg = 0.5 * bg_ref[h]
        g_bias = bin_ref[:, cols]
        hstate = state[:, cols]
        for t in range(tm // batch):
            rows = pl.ds(t * batch, batch)
            half_gates = 0.5 * gts[rows, :] + half_bg
            log_a = hs + hs * jnp.tanh(half_gates[:, 0:blk])
            half_xr = 0.5 * xr[rows, cols]
            gated_x = half_xr + half_xr * jnp.tanh(half_gates[:, blk:2 * blk])
            a = jnp.exp(log_a)
            one_minus_a2 = jnp.tanh(log_a) * (-1.0 - a * a)
            mult = one_minus_a2 * lax.rsqrt(jnp.maximum(one_minus_a2, F32_MIN_NORMAL))
            hstate = a * hstate + mult * gated_x
            gated[rows, cols] = hstate * _gelu_tanh(gbr[rows, cols] + g_bias)
        state[:, cols] = hstate
    for work in side_work[heads:]:
        work()

    zbuf[...] = _matmul(gated[...], wout_ref[...])

    for j in range(n_chunks):
        conv_chunk(j)
    for j in range(RGLRU_LEAD_G_CHUNKS):
        g_chunk(xn_ref, j)

    _residual_ln_store(x_ref, zbuf, g_ref, b_ref, o_ref, stats.at[0], stats.at[1], alpha, tm)


def _resident(shape):
    zeros = (0,) * len(shape)
    return pl.BlockSpec(shape, lambda s: zeros, pipeline_mode=pl.Buffered(1))


def _sublayer_call(body, name, x2, operands, scratch, tm):
    rows, d = x2.shape
    n_tiles = rows // tm
    cur_spec = pl.BlockSpec((tm, d), lambda s: (s, 0))
    next_spec = pl.BlockSpec((tm, d), lambda s: (jnp.minimum(s + 1, n_tiles - 1), 0))
    return pl.pallas_call(
        body,
        out_shape=jax.ShapeDtypeStruct((rows, d), _F32),
        grid=(n_tiles,),
        in_specs=[cur_spec, next_spec] + [_resident(op.shape) for op in operands],
        out_specs=cur_spec,
        scratch_shapes=scratch,
        compiler_params=pltpu.CompilerParams(
            dimension_semantics=("arbitrary",),
            vmem_limit_bytes=VMEM_LIMIT_BYTES),
        name=name,
    )(x2, x2, *operands)


def _row(v):
    return v.reshape(1, -1)


def _sconv_layer(x2, w_in, conv_w, conv_b, w_out, ln_g, ln_b, *, alpha, batch):
    tm = ROW_TILE
    d = x2.shape[1]
    assert d % COL_BLOCK == 0 and SCONV_LEAD_PAIRS <= d // COL_BLOCK
    hist = (conv_w.shape[0] - 1) * batch
    vm = pltpu.VMEM
    body = functools.partial(_sconv_kernel, alpha=alpha, batch=batch, tm=tm, d=d)
    return _sublayer_call(
        body, "sconv_mixer", x2,
        [w_in.astype(_BF16), conv_w, _row(conv_b), w_out.astype(_BF16), _row(ln_g), _row(ln_b)],
        [vm((tm, 3 * d), _F32), vm((hist + tm, d), _F32), vm((tm, d), _F32), vm((tm, d), _F32),
         vm((2, tm, LANES), _F32)],
        tm)


def _rglru_layer(x2, w_in, b_in, conv_w, conv_b, w_gate, b_gate, lam, w_out, ln_g, ln_b,
                 *, alpha, batch):
    tm = ROW_TILE
    d = x2.shape[1]
    r_width = w_out.shape[0]
    heads = w_gate.shape[0]
    blk = r_width // heads
    assert r_width % COL_BLOCK == 0 and COL_BLOCK % blk == 0
    assert RGLRU_LEAD_G_CHUNKS <= r_width // COL_BLOCK
    hist = (conv_w.shape[0] - 1) * batch
    vm = pltpu.VMEM
    body = functools.partial(_rglru_kernel, alpha=alpha, batch=batch, tm=tm,
                             r_width=r_width, heads=heads)
    wide = vm((tm, r_width), _F32)
    return _sublayer_call(
        body, "rglru_mixer", x2,
        [w_in.astype(_BF16), _row(b_in), conv_w, _row(conv_b), w_gate.astype(_BF16),
         b_gate.reshape(heads, 1, -1), _row(lam), w_out.astype(_BF16), _row(ln_g), _row(ln_b)],
        [vm((hist + tm, r_width), _F32), wide, wide, vm((2, tm, 2 * blk), _F32),
         vm((batch, r_width), _F32), wide, vm((tm, d), _F32), vm((2, tm, LANES), _F32)],
        tm)


def _ffn_layer(x2, w_up, conv_w, conv_b, w_down, ln_g, ln_b, *, alpha, batch):
    tm = ROW_TILE
    d = x2.shape[1]
    f = w_down.shape[0]
    assert f % COL_BLOCK == 0 and FFN_LEAD_PAIRS <= f // COL_BLOCK
    hist = (conv_w.shape[0] - 1) * batch
    vm = pltpu.VMEM
    body = functools.partial(_ffn_kernel, alpha=alpha, batch=batch, tm=tm, f=f)
    return _sublayer_call(
        body, "conv_ffn", x2,
        [w_up.astype(_BF16), conv_w, _row(conv_b), w_down.astype(_BF16), _row(ln_g), _row(ln_b)],
        [vm((hist + tm, 2 * f), _F32), vm((tm, f), _F32), vm((tm, d), _F32),
         vm((2, tm, LANES), _F32)],
        tm)


def kernel(x, sc_w_in, sc_conv_w, sc_conv_b, sc_w_out, lru_w_in, lru_b_in, lru_conv_w, lru_conv_b,
           lru_w_gate, lru_b_gate, lru_lambda, lru_w_out, ffn_w_up, ffn_conv_w, ffn_conv_b,
           ffn_w_down, ln_g, ln_b):
    batch, seq, d = x.shape
    depth = ffn_w_up.shape[0]
    alpha = (2.0 * depth) ** 0.25
    assert ROW_TILE % batch == 0 and (seq * batch) % ROW_TILE == 0 and batch % 8 == 0
    assert ROW_TILE % ROW_BLOCK == 0 and d % LANES == 0

    x2 = jnp.transpose(x, (1, 0, 2)).reshape(seq * batch, d)
    for i in range(depth):
        j = i // 2
        if i % 2 == 0:
            x2 = _sconv_layer(x2, sc_w_in[j], sc_conv_w[j], sc_conv_b[j], sc_w_out[j],
                              ln_g[i, 0], ln_b[i, 0], alpha=alpha, batch=batch)
        else:
            x2 = _rglru_layer(x2, lru_w_in[j], lru_b_in[j], lru_conv_w[j], lru_conv_b[j],
                              lru_w_gate[j], lru_b_gate[j], lru_lambda[j], lru_w_out[j],
                              ln_g[i, 0], ln_b[i, 0], alpha=alpha, batch=batch)
        x2 = _ffn_layer(x2, ffn_w_up[i], ffn_conv_w[i], ffn_conv_b[i], ffn_w_down[i],
                        ln_g[i, 1], ln_b[i, 1], alpha=alpha, batch=batch)
    return jnp.transpose(x2.reshape(seq, batch, d), (1, 0, 2))
```

```python
import functools
import math

import jax
import jax.numpy as jnp
from jax import lax
from jax.experimental import pallas as pl
from jax.experimental.pallas import tpu as pltpu

LN_EPS = 1e-5
LRU_C = 8.0
ROW_TILE = 512
ROW_BLOCK = 32
COL_BLOCK = 256
LANES = 128
LN_ROWS = 8
FFN_LEAD_PAIRS = 2
SCONV_LEAD_PAIRS = 2
RGLRU_EARLY_OUT_HEADS = 4
VMEM_LIMIT_BYTES = 56 * 1024 * 1024
F32_MIN_NORMAL = 1.1754943508222875e-38
_BF16 = jnp.bfloat16
_F32 = jnp.float32


def _gelu_tanh(z):
    c = math.sqrt(2.0 / math.pi)
    half = 0.5 * z
    return half + half * jnp.tanh(z * (c + (c * 0.044715) * (z * z)))


def _residual_ln_store(x_ref, z_ref, g_ref, b_ref, o_ref, mu_ref, rs_ref, alpha, tm, y2_ref=None):
    d = z_ref.shape[1]
    lanes = mu_ref.shape[1]
    inv_d = 1.0 / d
    nr = LN_ROWS

    def lane_total(r, fn):
        rows = pl.ds(r, nr)
        acc = fn(rows, 0)
        for c in range(lanes, d, lanes):
            acc = acc + fn(rows, c)
        return jnp.broadcast_to(jnp.sum(acc, axis=-1, keepdims=True), (nr, lanes))

    def residual_sum(rows, c):
        cols = slice(c, c + lanes)
        z = alpha * x_ref[rows, cols] + z_ref[rows, cols]
        if y2_ref is not None:
            z = z + y2_ref[rows, cols]
        z_ref[rows, cols] = z
        return z

    for r in range(0, tm, nr):
        mu_ref[pl.ds(r, nr), :] = inv_d * lane_total(r, residual_sum)

    def centred_sq(rows, c):
        cols = slice(c, c + lanes)
        zc = z_ref[rows, cols] - mu_ref[rows, :]
        z_ref[rows, cols] = zc
        return zc * zc

    for r in range(0, tm, nr):
        rs_ref[pl.ds(r, nr), :] = lax.rsqrt(inv_d * lane_total(r, centred_sq) + LN_EPS)

    for r in range(0, tm, nr):
        rows = pl.ds(r, nr)
        rs = rs_ref[rows, :]
        for c in range(0, d, lanes):
            cols = slice(c, c + lanes)
            o_ref[rows, cols] = z_ref[rows, cols] * rs * g_ref[:, cols] + b_ref[:, cols]


def _causal_conv(buf_ref, r, cols, w, bias, hist, batch):
    k_width = w.shape[0]
    acc = bias
    for k in range(k_width):
        off = hist - (k_width - 1 - k) * batch
        acc = acc + buf_ref[pl.ds(r + off, ROW_BLOCK), cols] * w[k:k + 1, :]
    return acc


def _matmul(lhs_f32, rhs_bf16):
    return lax.dot_general(lhs_f32, rhs_bf16, (((1,), (0,)), ((), ())),
                           preferred_element_type=_F32)


def _project(x_ref, w_ref, cols):
    return _matmul(x_ref[...], w_ref[:, cols])


def _ffn_kernel(x_ref, xn_ref, wup_ref, cw_ref, cb_ref, wdn_ref, g_ref, b_ref, o_ref,
                hbuf, act, zbuf, stats, *, alpha, batch, tm, f):
    hist = (cw_ref.shape[0] - 1) * batch
    fc = COL_BLOCK
    n_pairs = f // fc
    cur = slice(hist, hist + tm)

    def up_pair(rows_ref, p):
        for c in (p * fc, f + p * fc):
            hbuf[cur, c:c + fc] = _project(rows_ref, wup_ref, slice(c, c + fc))

    @pl.when(pl.program_id(0) == 0)
    def _():
        hbuf[0:hist, :] = jnp.zeros((hist, hbuf.shape[1]), _F32)
        for p in range(FFN_LEAD_PAIRS):
            up_pair(x_ref, p)

    for p in range(FFN_LEAD_PAIRS, n_pairs):
        up_pair(x_ref, p)

    for p in range(n_pairs):
        gcols = slice(p * fc, (p + 1) * fc)
        vcols = slice(f + p * fc, f + (p + 1) * fc)
        wg, bg = cw_ref[:, gcols], cb_ref[:, gcols]
        wv, bv = cw_ref[:, vcols], cb_ref[:, vcols]
        for r in range(0, tm, ROW_BLOCK):
            gate = _causal_conv(hbuf, r, gcols, wg, bg, hist, batch)
            val = _causal_conv(hbuf, r, vcols, wv, bv, hist, batch)
            half = 0.5 * gate
            act[pl.ds(r, ROW_BLOCK), gcols] = (half + half * jnp.tanh(half)) * val
        for cols in (gcols, vcols):
            hbuf[0:hist, cols] = hbuf[tm:tm + hist, cols]

    zbuf[...] = _matmul(act[...], wdn_ref[...])

    _residual_ln_store(x_ref, zbuf, g_ref, b_ref, o_ref, stats.at[0], stats.at[1], alpha, tm)

    for p in range(FFN_LEAD_PAIRS):
        up_pair(xn_ref, p)


def _sconv_kernel(x_ref, xn_ref, win_ref, cw_ref, cb_ref, wout_ref, g_ref, b_ref, o_ref,
                  hbuf, cvbuf, gated, zbuf, stats, *, alpha, batch, tm, d):
    hist = (cw_ref.shape[0] - 1) * batch
    dc = COL_BLOCK
    n_chunks = d // dc

    def cv_pair(rows_ref, p):
        for c in (d + p * dc, 2 * d + p * dc):
            hbuf[:, c:c + dc] = _project(rows_ref, win_ref, slice(c, c + dc))

    @pl.when(pl.program_id(0) == 0)
    def _():
        cvbuf[0:hist, :] = jnp.zeros((hist, d), _F32)
        for p in range(SCONV_LEAD_PAIRS):
            cv_pair(x_ref, p)

    for p in range(SCONV_LEAD_PAIRS, n_chunks):
        cv_pair(x_ref, p)
    for p in range(n_chunks):
        hbuf[:, p * dc:(p + 1) * dc] = _project(x_ref, win_ref, slice(p * dc, (p + 1) * dc))

    for p in range(n_chunks):
        c = p * dc
        for r in range(0, tm, ROW_BLOCK):
            rows = pl.ds(r, ROW_BLOCK)
            cvbuf[pl.ds(hist + r, ROW_BLOCK), c:c + dc] = (
                hbuf[rows, d + c:d + c + dc] * hbuf[rows, 2 * d + c:2 * d + c + dc])
    for p in range(n_chunks):
        cols = slice(p * dc, (p + 1) * dc)
        w, bias = cw_ref[:, cols], cb_ref[:, cols]
        for r in range(0, tm, ROW_BLOCK):
            rows = pl.ds(r, ROW_BLOCK)
            u = _causal_conv(cvbuf, r, cols, w, bias, hist, batch)
            gated[rows, cols] = hbuf[rows, cols] * u
        cvbuf[0:hist, cols] = cvbuf[tm:tm + hist, cols]

    zbuf[...] = _matmul(gated[...], wout_ref[...])

    for p in range(SCONV_LEAD_PAIRS):
        cv_pair(xn_ref, p)

    _residual_ln_store(x_ref, zbuf, g_ref, b_ref, o_ref, stats.at[0], stats.at[1], alpha, tm)


def _rglru_kernel(x_ref, xn_ref, win_ref, bin_ref, cw_ref, cb_ref, wg_ref, bg_ref, lam_ref,
                  wout_ref, g_ref, b_ref, o_ref,
                  rbuf, gbr, xr, gates, state, gated, zbuf, zearly, stats,
                  *, alpha, batch, tm, r_width, heads):
    hist = (cw_ref.shape[0] - 1) * batch
    blk = r_width // heads
    ic = COL_BLOCK
    n_chunks = r_width // ic
    cur = slice(hist, hist + tm)

    w = 0.5 * cw_ref[...]
    w_sum = w[0:1, :]
    for k in range(1, w.shape[0]):
        w_sum = w_sum + w[k:k + 1, :]
    conv_bias = 0.5 * cb_ref[...] + bin_ref[:, r_width:2 * r_width] * w_sum

    def r_chunk(rows_ref, j):
        rbuf[cur, j * ic:(j + 1) * ic] = _project(
            rows_ref, win_ref, slice(r_width + j * ic, r_width + (j + 1) * ic))

    def g_chunk(rows_ref, j):
        gbr[:, j * ic:(j + 1) * ic] = _project(rows_ref, win_ref, slice(j * ic, (j + 1) * ic))

    def conv_chunk(j):
        cols = slice(j * ic, (j + 1) * ic)
        wc, bc = w[:, cols], conv_bias[:, cols]
        for r in range(0, tm, ROW_BLOCK):
            rows = pl.ds(r, ROW_BLOCK)
            xr[rows, cols] = _causal_conv(rbuf, r, cols, wc, bc, hist, batch)
        rbuf[0:hist, cols] = rbuf[tm:tm + hist, cols]

    @pl.when(pl.program_id(0) == 0)
    def _():
        rbuf[0:hist, :] = jnp.broadcast_to(-bin_ref[:, r_width:2 * r_width], (hist, r_width))
        state[...] = jnp.zeros(state.shape, _F32)
        for j in range(n_chunks):
            r_chunk(x_ref, j)
        for j in range(n_chunks):
            conv_chunk(j)

    neg_lam = -lam_ref[...]
    softplus = jnp.maximum(neg_lam, 0.0) + jnp.log1p(jnp.exp(-jnp.abs(neg_lam)))
    half_scale = (-0.5 * LRU_C) * softplus

    k_split = RGLRU_EARLY_OUT_HEADS * blk
    for h in range(heads):
        cols = slice(h * blk, (h + 1) * blk)
        gts = gates.at[h % 2]

        gts[...] = _matmul(xr[:, cols], wg_ref[h])
        if h < n_chunks:
            g_chunk(x_ref, h)
        if h == min(RGLRU_EARLY_OUT_HEADS + 1, heads - 1):
            zearly[...] = _matmul(gated[:, 0:k_split], wout_ref[0:k_split, :])

        hs = half_scale[:, cols]
        half_bg = 0.5 * bg_ref[h]
        g_bias = bin_ref[:, cols]
        hstate = state[:, cols]
        for t in range(tm // batch):
            rows = pl.ds(t * batch, batch)
            half_gates = gts[rows, :] + half_bg
            log_a = hs + hs * jnp.tanh(half_gates[:, 0:blk])
            half_xr = xr[rows, cols]
            gated_x = half_xr + half_xr * jnp.tanh(half_gates[:, blk:2 * blk])
            a = jnp.exp(log_a)
            one_minus_a2 = jnp.tanh(log_a) * (-1.0 - a * a)
            mult = one_minus_a2 * lax.rsqrt(jnp.maximum(one_minus_a2, F32_MIN_NORMAL))
            hstate = a * hstate + mult * gated_x
            gated[rows, cols] = hstate * _gelu_tanh(gbr[rows, cols] + g_bias)
        state[:, cols] = hstate

    zbuf[...] = _matmul(gated[:, k_split:r_width], wout_ref[k_split:r_width, :])

    for j in range(n_chunks):
        r_chunk(xn_ref, j)
    for j in range(n_chunks):
        conv_chunk(j)

    _residual_ln_store(x_ref, zbuf, g_ref, b_ref, o_ref, stats.at[0], stats.at[1], alpha, tm,
                       y2_ref=zearly)


def _layer(stacked, *index):
    return stacked, index


def _resident(operand):
    stacked, index = operand
    block = (None,) * len(index) + stacked.shape[len(index):]
    origin = index + (0,) * (stacked.ndim - len(index))
    return pl.BlockSpec(block, lambda s: origin, pipeline_mode=pl.Buffered(1))


def _sublayer_call(body, name, x2, operands, scratch, tm):
    rows, d = x2.shape
    n_tiles = rows // tm
    cur_spec = pl.BlockSpec((tm, d), lambda s: (s, 0))
    next_spec = pl.BlockSpec((tm, d), lambda s: (jnp.minimum(s + 1, n_tiles - 1), 0))
    return pl.pallas_call(
        body,
        out_shape=jax.ShapeDtypeStruct((rows, d), _F32),
        grid=(n_tiles,),
        in_specs=[cur_spec, next_spec] + [_resident(op) for op in operands],
        out_specs=cur_spec,
        scratch_shapes=scratch,
        compiler_params=pltpu.CompilerParams(
            dimension_semantics=("arbitrary",),
            vmem_limit_bytes=VMEM_LIMIT_BYTES),
        name=name,
    )(x2, x2, *[stacked for stacked, _ in operands])


def _rows(v):
    return v.reshape(v.shape[:-1] + (1, v.shape[-1]))


def _sconv_layer(x2, p, j, ln, *, alpha, batch):
    tm = ROW_TILE
    d = x2.shape[1]
    assert d % COL_BLOCK == 0 and SCONV_LEAD_PAIRS <= d // COL_BLOCK
    hist = (p["conv_w"].shape[1] - 1) * batch
    vm = pltpu.VMEM
    body = functools.partial(_sconv_kernel, alpha=alpha, batch=batch, tm=tm, d=d)
    return _sublayer_call(
        body, "sconv_mixer", x2,
        [_layer(p["w_in"], j), _layer(p["conv_w"], j), _layer(p["conv_b"], j),
         _layer(p["w_out"], j), _layer(ln["g"], *ln["at"]), _layer(ln["b"], *ln["at"])],
        [vm((tm, 3 * d), _F32), vm((hist + tm, d), _F32), vm((tm, d), _F32), vm((tm, d), _F32),
         vm((2, tm, LANES), _F32)],
        tm)


def _rglru_layer(x2, p, j, ln, *, alpha, batch):
    tm = ROW_TILE
    d = x2.shape[1]
    r_width = p["w_out"].shape[1]
    heads = p["w_gate"].shape[1]
    blk = r_width // heads
    assert r_width % COL_BLOCK == 0 and COL_BLOCK % blk == 0
    assert (RGLRU_EARLY_OUT_HEADS * blk) % COL_BLOCK == 0 and RGLRU_EARLY_OUT_HEADS < heads
    hist = (p["conv_w"].shape[1] - 1) * batch
    vm = pltpu.VMEM
    body = functools.partial(_rglru_kernel, alpha=alpha, batch=batch, tm=tm,
                             r_width=r_width, heads=heads)
    wide = vm((tm, r_width), _F32)
    return _sublayer_call(
        body, "rglru_mixer", x2,
        [_layer(p["w_in"], j), _layer(p["b_in"], j), _layer(p["conv_w"], j),
         _layer(p["conv_b"], j), _layer(p["w_gate"], j), _layer(p["b_gate"], j),
         _layer(p["lam"], j), _layer(p["w_out"], j),
         _layer(ln["g"], *ln["at"]), _layer(ln["b"], *ln["at"])],
        [vm((hist + tm, r_width), _F32), wide, wide, vm((2, tm, 2 * blk), _F32),
         vm((batch, r_width), _F32), wide, vm((tm, d), _F32), vm((tm, d), _F32),
         vm((2, tm, LANES), _F32)],
        tm)


def _ffn_layer(x2, p, i, ln, *, alpha, batch):
    tm = ROW_TILE
    d = x2.shape[1]
    f = p["w_down"].shape[1]
    assert f % COL_BLOCK == 0 and FFN_LEAD_PAIRS <= f // COL_BLOCK
    hist = (p["conv_w"].shape[1] - 1) * batch
    vm = pltpu.VMEM
    body = functools.partial(_ffn_kernel, alpha=alpha, batch=batch, tm=tm, f=f)
    return _sublayer_call(
        body, "conv_ffn", x2,
        [_layer(p["w_up"], i), _layer(p["conv_w"], i), _layer(p["conv_b"], i),
         _layer(p["w_down"], i), _layer(ln["g"], *ln["at"]), _layer(ln["b"], *ln["at"])],
        [vm((hist + tm, 2 * f), _F32), vm((tm, f), _F32), vm((tm, d), _F32),
         vm((2, tm, LANES), _F32)],
        tm)


def kernel(x, sc_w_in, sc_conv_w, sc_conv_b, sc_w_out, lru_w_in, lru_b_in, lru_conv_w, lru_conv_b,
           lru_w_gate, lru_b_gate, lru_lambda, lru_w_out, ffn_w_up, ffn_conv_w, ffn_conv_b,
           ffn_w_down, ln_g, ln_b):
    batch, seq, d = x.shape
    depth = ffn_w_up.shape[0]
    alpha = (2.0 * depth) ** 0.25
    assert ROW_TILE % batch == 0 and (seq * batch) % ROW_TILE == 0 and batch % 8 == 0
    assert ROW_TILE % ROW_BLOCK == 0 and d % LANES == 0

    sconv = dict(w_in=sc_w_in.astype(_BF16), conv_w=sc_conv_w, conv_b=_rows(sc_conv_b),
                 w_out=sc_w_out.astype(_BF16))
    rglru = dict(w_in=lru_w_in.astype(_BF16), b_in=_rows(lru_b_in), conv_w=lru_conv_w,
                 conv_b=_rows(lru_conv_b), w_gate=lru_w_gate.astype(_BF16),
                 b_gate=_rows(lru_b_gate), lam=_rows(lru_lambda), w_out=lru_w_out.astype(_BF16))
    ffn = dict(w_up=ffn_w_up.astype(_BF16), conv_w=ffn_conv_w, conv_b=_rows(ffn_conv_b),
               w_down=ffn_w_down.astype(_BF16))
    ln_g, ln_b = _rows(ln_g), _rows(ln_b)

    x2 = jnp.transpose(x, (1, 0, 2)).reshape(seq * batch, d)
    for i in range(depth):
        j = i // 2
        ln = dict(g=ln_g, b=ln_b, at=(i, 0))
        if i % 2 == 0:
            x2 = _sconv_layer(x2, sconv, j, ln, alpha=alpha, batch=batch)
        else:
            x2 = _rglru_layer(x2, rglru, j, ln, alpha=alpha, batch=batch)
        ln = dict(g=ln_g, b=ln_b, at=(i, 1))
        x2 = _ffn_layer(x2, ffn, i, ln, alpha=alpha, batch=batch)
    return jnp.transpose(x2.reshape(seq, batch, d), (1, 0, 2))
```

```python
import functools
import math

import jax
import jax.numpy as jnp
from jax import lax
from jax.experimental import pallas as pl
from jax.experimental.pallas import tpu as pltpu

LN_EPS = 1e-5
LRU_C = 8.0
ROW_TILE = 512
ROW_BLOCK = 32
COL_BLOCK = 256
LANES = 128
LN_ROWS = 8
FFN_LEAD_PAIRS = 2
SCONV_LEAD_PAIRS = 2
RGLRU_EARLY_OUT_HEADS = 4
VMEM_LIMIT_BYTES = 56 * 1024 * 1024
F32_MIN_NORMAL = 1.1754943508222875e-38
_BF16 = jnp.bfloat16
_F32 = jnp.float32


def _gelu_tanh(z):
    c = math.sqrt(2.0 / math.pi)
    half = 0.5 * z
    return half + half * jnp.tanh(z * (c + (c * 0.044715) * (z * z)))


def _residual_ln_store(x_ref, z_ref, g_ref, b_ref, o_ref, mu_ref, rs_ref, alpha, tm, y2_ref=None):
    d = z_ref.shape[1]
    lanes = mu_ref.shape[1]
    inv_d = 1.0 / d
    nr = LN_ROWS

    def lane_total(r, fn):
        rows = pl.ds(r, nr)
        acc = fn(rows, 0)
        for c in range(lanes, d, lanes):
            acc = acc + fn(rows, c)
        return jnp.broadcast_to(jnp.sum(acc, axis=-1, keepdims=True), (nr, lanes))

    def residual_sum(rows, c):
        cols = slice(c, c + lanes)
        z = alpha * x_ref[rows, cols] + z_ref[rows, cols]
        if y2_ref is not None:
            z = z + y2_ref[rows, cols]
        z_ref[rows, cols] = z
        return z

    for r in range(0, tm, nr):
        mu_ref[pl.ds(r, nr), :] = inv_d * lane_total(r, residual_sum)

    def centred_sq(rows, c):
        cols = slice(c, c + lanes)
        zc = z_ref[rows, cols] - mu_ref[rows, :]
        z_ref[rows, cols] = zc
        return zc * zc

    for r in range(0, tm, nr):
        rs_ref[pl.ds(r, nr), :] = lax.rsqrt(inv_d * lane_total(r, centred_sq) + LN_EPS)

    for r in range(0, tm, nr):
        rows = pl.ds(r, nr)
        rs = rs_ref[rows, :]
        for c in range(0, d, lanes):
            cols = slice(c, c + lanes)
            o_ref[rows, cols] = z_ref[rows, cols] * rs * g_ref[:, cols] + b_ref[:, cols]


def _causal_conv(buf_ref, r, cols, w, bias, hist, batch):
    k_width = w.shape[0]
    acc = bias
    for k in range(k_width):
        off = hist - (k_width - 1 - k) * batch
        acc = acc + buf_ref[pl.ds(r + off, ROW_BLOCK), cols] * w[k:k + 1, :]
    return acc


def _matmul(lhs_f32, rhs_bf16):
    return lax.dot_general(lhs_f32, rhs_bf16, (((1,), (0,)), ((), ())),
                           preferred_element_type=_F32)


def _project(x_ref, w_ref, cols):
    return _matmul(x_ref[...], w_ref[:, cols])


def _ffn_kernel(x_ref, xn_ref, wup_ref, cw_ref, cb_ref, wdn_ref, g_ref, b_ref, o_ref,
                hbuf, act, zbuf, stats, *out_scratch, alpha, batch, tm, f, batch_major_out):
    hist = (cw_ref.shape[0] - 1) * batch
    fc = COL_BLOCK
    n_pairs = f // fc
    cur = slice(hist, hist + tm)
    step = pl.program_id(0)

    def up_pair(rows_ref, p):
        for c in (p * fc, f + p * fc):
            hbuf[cur, c:c + fc] = _project(rows_ref, wup_ref, slice(c, c + fc))

    @pl.when(pl.program_id(0) == 0)
    def _():
        hbuf[0:hist, :] = jnp.zeros((hist, hbuf.shape[1]), _F32)
        for p in range(FFN_LEAD_PAIRS):
            up_pair(x_ref, p)

    for p in range(FFN_LEAD_PAIRS, n_pairs):
        up_pair(x_ref, p)

    for p in range(n_pairs):
        gcols = slice(p * fc, (p + 1) * fc)
        vcols = slice(f + p * fc, f + (p + 1) * fc)
        wg, bg = cw_ref[:, gcols], cb_ref[:, gcols]
        wv, bv = cw_ref[:, vcols], cb_ref[:, vcols]
        for r in range(0, tm, ROW_BLOCK):
            gate = _causal_conv(hbuf, r, gcols, wg, bg, hist, batch)
            val = _causal_conv(hbuf, r, vcols, wv, bv, hist, batch)
            half = 0.5 * gate
            act[pl.ds(r, ROW_BLOCK), gcols] = (half + half * jnp.tanh(half)) * val
        for cols in (gcols, vcols):
            hbuf[0:hist, cols] = hbuf[tm:tm + hist, cols]

    zbuf[...] = _matmul(act[...], wdn_ref[...])

    if batch_major_out:
        rows_out, sems = out_scratch
        slot = step % 2
        ln_out = rows_out.at[slot]
    else:
        ln_out = o_ref
    _residual_ln_store(x_ref, zbuf, g_ref, b_ref, ln_out, stats.at[0], stats.at[1], alpha, tm)

    for p in range(FFN_LEAD_PAIRS):
        up_pair(xn_ref, p)

    if batch_major_out:
        steps_per_tile = tm // batch

        def slab_copy(of_step, of_slot, t):
            return pltpu.make_async_copy(
                rows_out.at[of_slot, pl.ds(t * batch, batch), :],
                o_ref.at[:, of_step * steps_per_tile + t, :], sems.at[of_slot])

        for t in range(steps_per_tile):
            slab_copy(step, slot, t).start()

        @pl.when(step > 0)
        def _():
            for t in range(steps_per_tile):
                slab_copy(step - 1, 1 - slot, t).wait()

        @pl.when(step == pl.num_programs(0) - 1)
        def _():
            for t in range(steps_per_tile):
                slab_copy(step, slot, t).wait()


def _sconv_kernel(x_ref, xn_ref, win_ref, cw_ref, cb_ref, wout_ref, g_ref, b_ref, o_ref,
                  hbuf, cvbuf, gated, zbuf, stats, *, alpha, batch, tm, d):
    hist = (cw_ref.shape[0] - 1) * batch
    dc = COL_BLOCK
    n_chunks = d // dc

    def cv_pair(rows_ref, p):
        for c in (d + p * dc, 2 * d + p * dc):
            hbuf[:, c:c + dc] = _project(rows_ref, win_ref, slice(c, c + dc))

    @pl.when(pl.program_id(0) == 0)
    def _():
        cvbuf[0:hist, :] = jnp.zeros((hist, d), _F32)
        for p in range(SCONV_LEAD_PAIRS):
            cv_pair(x_ref, p)

    for p in range(SCONV_LEAD_PAIRS, n_chunks):
        cv_pair(x_ref, p)
    for p in range(n_chunks):
        hbuf[:, p * dc:(p + 1) * dc] = _project(x_ref, win_ref, slice(p * dc, (p + 1) * dc))

    for p in range(n_chunks):
        c = p * dc
        for r in range(0, tm, ROW_BLOCK):
            rows = pl.ds(r, ROW_BLOCK)
            cvbuf[pl.ds(hist + r, ROW_BLOCK), c:c + dc] = (
                hbuf[rows, d + c:d + c + dc] * hbuf[rows, 2 * d + c:2 * d + c + dc])
    for p in range(n_chunks):
        cols = slice(p * dc, (p + 1) * dc)
        w, bias = cw_ref[:, cols], cb_ref[:, cols]
        for r in range(0, tm, ROW_BLOCK):
            rows = pl.ds(r, ROW_BLOCK)
            u = _causal_conv(cvbuf, r, cols, w, bias, hist, batch)
            gated[rows, cols] = hbuf[rows, cols] * u
        cvbuf[0:hist, cols] = cvbuf[tm:tm + hist, cols]

    zbuf[...] = _matmul(gated[...], wout_ref[...])

    for p in range(SCONV_LEAD_PAIRS):
        cv_pair(xn_ref, p)

    _residual_ln_store(x_ref, zbuf, g_ref, b_ref, o_ref, stats.at[0], stats.at[1], alpha, tm)


def _rglru_kernel(x_ref, xn_ref, win_ref, bin_ref, cw_ref, cb_ref, wg_ref, bg_ref, lam_ref,
                  wout_ref, g_ref, b_ref, o_ref,
                  rbuf, gbr, xr, gates, state, gated, zbuf, zearly, stats,
                  *, alpha, batch, tm, r_width, heads):
    hist = (cw_ref.shape[0] - 1) * batch
    blk = r_width // heads
    ic = COL_BLOCK
    n_chunks = r_width // ic
    cur = slice(hist, hist + tm)

    w = 0.5 * cw_ref[...]
    w_sum = w[0:1, :]
    for k in range(1, w.shape[0]):
        w_sum = w_sum + w[k:k + 1, :]
    conv_bias = 0.5 * cb_ref[...] + bin_ref[:, r_width:2 * r_width] * w_sum

    def r_chunk(rows_ref, j):
        rbuf[cur, j * ic:(j + 1) * ic] = _project(
            rows_ref, win_ref, slice(r_width + j * ic, r_width + (j + 1) * ic))

    def g_chunk(rows_ref, j):
        gbr[:, j * ic:(j + 1) * ic] = _project(rows_ref, win_ref, slice(j * ic, (j + 1) * ic))

    def conv_chunk(j):
        cols = slice(j * ic, (j + 1) * ic)
        wc, bc = w[:, cols], conv_bias[:, cols]
        for r in range(0, tm, ROW_BLOCK):
            rows = pl.ds(r, ROW_BLOCK)
            xr[rows, cols] = _causal_conv(rbuf, r, cols, wc, bc, hist, batch)
        rbuf[0:hist, cols] = rbuf[tm:tm + hist, cols]

    @pl.when(pl.program_id(0) == 0)
    def _():
        rbuf[0:hist, :] = jnp.broadcast_to(-bin_ref[:, r_width:2 * r_width], (hist, r_width))
        state[...] = jnp.zeros(state.shape, _F32)
        for j in range(n_chunks):
            r_chunk(x_ref, j)
        for j in range(n_chunks):
            conv_chunk(j)

    neg_lam = -lam_ref[...]
    softplus = jnp.maximum(neg_lam, 0.0) + jnp.log1p(jnp.exp(-jnp.abs(neg_lam)))
    half_scale = (-0.5 * LRU_C) * softplus

    k_split = RGLRU_EARLY_OUT_HEADS * blk
    for h in range(heads):
        cols = slice(h * blk, (h + 1) * blk)
        gts = gates.at[h % 2]

        gts[...] = _matmul(xr[:, cols], wg_ref[h])
        if h < n_chunks:
            g_chunk(x_ref, h)
        if h == min(RGLRU_EARLY_OUT_HEADS + 1, heads - 1):
            zearly[...] = _matmul(gated[:, 0:k_split], wout_ref[0:k_split, :])

        hs = half_scale[:, cols]
        half_bg = 0.5 * bg_ref[h]
        g_bias = bin_ref[:, cols]
        hstate = state[:, cols]
        for t in range(tm // batch):
            rows = pl.ds(t * batch, batch)
            half_gates = gts[rows, :] + half_bg
            log_a = hs + hs * jnp.tanh(half_gates[:, 0:blk])
            half_xr = xr[rows, cols]
            gated_x = half_xr + half_xr * jnp.tanh(half_gates[:, blk:2 * blk])
            a = jnp.exp(log_a)
            one_minus_a2 = jnp.tanh(log_a) * (-1.0 - a * a)
            mult = one_minus_a2 * lax.rsqrt(jnp.maximum(one_minus_a2, F32_MIN_NORMAL))
            hstate = a * hstate + mult * gated_x
            gated[rows, cols] = hstate * _gelu_tanh(gbr[rows, cols] + g_bias)
        state[:, cols] = hstate

    zbuf[...] = _matmul(gated[:, k_split:r_width], wout_ref[k_split:r_width, :])

    for j in range(n_chunks):
        r_chunk(xn_ref, j)
    for j in range(n_chunks):
        conv_chunk(j)

    _residual_ln_store(x_ref, zbuf, g_ref, b_ref, o_ref, stats.at[0], stats.at[1], alpha, tm,
                       y2_ref=zearly)


def _layer(stacked, *index):
    return stacked, index


def _resident(operand):
    stacked, index = operand
    block = (None,) * len(index) + stacked.shape[len(index):]
    origin = index + (0,) * (stacked.ndim - len(index))
    return pl.BlockSpec(block, lambda s: origin, pipeline_mode=pl.Buffered(1))


def _sublayer_call(body, name, x2, operands, scratch, tm, out_batch=None):
    rows, d = x2.shape
    n_tiles = rows // tm
    cur_spec = pl.BlockSpec((tm, d), lambda s: (s, 0))
    next_spec = pl.BlockSpec((tm, d), lambda s: (jnp.minimum(s + 1, n_tiles - 1), 0))
    if out_batch is None:
        out_shape, out_spec = jax.ShapeDtypeStruct((rows, d), _F32), cur_spec
    else:
        out_shape = jax.ShapeDtypeStruct((out_batch, rows // out_batch, d), _F32)
        out_spec = pl.BlockSpec(memory_space=pl.ANY)
    return pl.pallas_call(
        body,
        out_shape=out_shape,
        grid=(n_tiles,),
        in_specs=[cur_spec, next_spec] + [_resident(op) for op in operands],
        out_specs=out_spec,
        scratch_shapes=scratch,
        compiler_params=pltpu.CompilerParams(
            dimension_semantics=("arbitrary",),
            vmem_limit_bytes=VMEM_LIMIT_BYTES),
        name=name,
    )(x2, x2, *[stacked for stacked, _ in operands])


def _rows(v):
    return v.reshape(v.shape[:-1] + (1, v.shape[-1]))


def _sconv_layer(x2, p, j, ln, *, alpha, batch):
    tm = ROW_TILE
    d = x2.shape[1]
    assert d % COL_BLOCK == 0 and SCONV_LEAD_PAIRS <= d // COL_BLOCK
    hist = (p["conv_w"].shape[1] - 1) * batch
    vm = pltpu.VMEM
    body = functools.partial(_sconv_kernel, alpha=alpha, batch=batch, tm=tm, d=d)
    return _sublayer_call(
        body, "sconv_mixer", x2,
        [_layer(p["w_in"], j), _layer(p["conv_w"], j), _layer(p["conv_b"], j),
         _layer(p["w_out"], j), _layer(ln["g"], *ln["at"]), _layer(ln["b"], *ln["at"])],
        [vm((tm, 3 * d), _F32), vm((hist + tm, d), _F32), vm((tm, d), _F32), vm((tm, d), _F32),
         vm((2, tm, LANES), _F32)],
        tm)


def _rglru_layer(x2, p, j, ln, *, alpha, batch):
    tm = ROW_TILE
    d = x2.shape[1]
    r_width = p["w_out"].shape[1]
    heads = p["w_gate"].shape[1]
    blk = r_width // heads
    assert r_width % COL_BLOCK == 0 and COL_BLOCK % blk == 0
    assert (RGLRU_EARLY_OUT_HEADS * blk) % COL_BLOCK == 0 and RGLRU_EARLY_OUT_HEADS < heads
    hist = (p["conv_w"].shape[1] - 1) * batch
    vm = pltpu.VMEM
    body = functools.partial(_rglru_kernel, alpha=alpha, batch=batch, tm=tm,
                             r_width=r_width, heads=heads)
    wide = vm((tm, r_width), _F32)
    return _sublayer_call(
        body, "rglru_mixer", x2,
        [_layer(p["w_in"], j), _layer(p["b_in"], j), _layer(p["conv_w"], j),
         _layer(p["conv_b"], j), _layer(p["w_gate"], j), _layer(p["b_gate"], j),
         _layer(p["lam"], j), _layer(p["w_out"], j),
         _layer(ln["g"], *ln["at"]), _layer(ln["b"], *ln["at"])],
        [vm((hist + tm, r_width), _F32), wide, wide, vm((2, tm, 2 * blk), _F32),
         vm((batch, r_width), _F32), wide, vm((tm, d), _F32), vm((tm, d), _F32),
         vm((2, tm, LANES), _F32)],
        tm)


def _ffn_layer(x2, p, i, ln, *, alpha, batch, batch_major_out=False):
    tm = ROW_TILE
    d = x2.shape[1]
    f = p["w_down"].shape[1]
    assert f % COL_BLOCK == 0 and FFN_LEAD_PAIRS <= f // COL_BLOCK
    hist = (p["conv_w"].shape[1] - 1) * batch
    vm = pltpu.VMEM
    body = functools.partial(_ffn_kernel, alpha=alpha, batch=batch, tm=tm, f=f,
                             batch_major_out=batch_major_out)
    scratch = [vm((hist + tm, 2 * f), _F32), vm((tm, f), _F32), vm((tm, d), _F32),
               vm((2, tm, LANES), _F32)]
    if batch_major_out:
        scratch += [vm((2, tm, d), _F32), pltpu.SemaphoreType.DMA((2,))]
    return _sublayer_call(
        body, "conv_ffn_out" if batch_major_out else "conv_ffn", x2,
        [_layer(p["w_up"], i), _layer(p["conv_w"], i), _layer(p["conv_b"], i),
         _layer(p["w_down"], i), _layer(ln["g"], *ln["at"]), _layer(ln["b"], *ln["at"])],
        scratch, tm, out_batch=batch if batch_major_out else None)


def kernel(x, sc_w_in, sc_conv_w, sc_conv_b, sc_w_out, lru_w_in, lru_b_in, lru_conv_w, lru_conv_b,
           lru_w_gate, lru_b_gate, lru_lambda, lru_w_out, ffn_w_up, ffn_conv_w, ffn_conv_b,
           ffn_w_down, ln_g, ln_b):
    batch, seq, d = x.shape
    depth = ffn_w_up.shape[0]
    alpha = (2.0 * depth) ** 0.25
    assert ROW_TILE % batch == 0 and (seq * batch) % ROW_TILE == 0 and batch % 8 == 0
    assert ROW_TILE % ROW_BLOCK == 0 and d % LANES == 0

    sconv = dict(w_in=sc_w_in.astype(_BF16), conv_w=sc_conv_w, conv_b=_rows(sc_conv_b),
                 w_out=sc_w_out.astype(_BF16))
    rglru = dict(w_in=lru_w_in.astype(_BF16), b_in=_rows(lru_b_in), conv_w=lru_conv_w,
                 conv_b=_rows(lru_conv_b), w_gate=lru_w_gate.astype(_BF16),
                 b_gate=_rows(lru_b_gate), lam=_rows(lru_lambda), w_out=lru_w_out.astype(_BF16))
    ffn = dict(w_up=ffn_w_up.astype(_BF16), conv_w=ffn_conv_w, conv_b=_rows(ffn_conv_b),
               w_down=ffn_w_down.astype(_BF16))
    ln_g, ln_b = _rows(ln_g), _rows(ln_b)

    x2 = jnp.transpose(x, (1, 0, 2)).reshape(seq * batch, d)
    for i in range(depth):
        j = i // 2
        ln = dict(g=ln_g, b=ln_b, at=(i, 0))
        if i % 2 == 0:
            x2 = _sconv_layer(x2, sconv, j, ln, alpha=alpha, batch=batch)
        else:
            x2 = _rglru_layer(x2, rglru, j, ln, alpha=alpha, batch=batch)
        ln = dict(g=ln_g, b=ln_b, at=(i, 1))
        x2 = _ffn_layer(x2, ffn, i, ln, alpha=alpha, batch=batch,
                        batch_major_out=(i == depth - 1))
    return x2
```

```python
import functools
import math

import jax
import jax.numpy as jnp
from jax import lax
from jax.experimental import pallas as pl
from jax.experimental.pallas import tpu as pltpu

LN_EPS = 1e-5
LRU_C = 8.0
ROW_TILE = 512
ROW_BLOCK = 32
COL_BLOCK = 256
LANES = 128
LN_ROWS = 8
FFN_LEAD_PAIRS = 2
SCONV_LEAD_PAIRS = 2
RGLRU_EARLY_OUT_HEADS = 4
RGLRU_TAIL_R_CHUNKS = 0
VMEM_LIMIT_BYTES = 56 * 1024 * 1024
F32_MIN_NORMAL = 1.1754943508222875e-38
_BF16 = jnp.bfloat16
_F32 = jnp.float32


def _twice_gelu_tanh(z):
    c = math.sqrt(2.0 / math.pi)
    return z + z * jnp.tanh(z * (c + (c * 0.044715) * (z * z)))


def _residual_ln_store(x_ref, z_ref, g_ref, b_ref, o_ref, mu_ref, rs_ref, alpha, tm, y2_ref=None):
    d = z_ref.shape[1]
    lanes = mu_ref.shape[1]
    inv_d = 1.0 / d
    nr = LN_ROWS

    def lane_total(r, fn):
        rows = pl.ds(r, nr)
        acc = fn(rows, 0)
        for c in range(lanes, d, lanes):
            acc = acc + fn(rows, c)
        return jnp.broadcast_to(jnp.sum(acc, axis=-1, keepdims=True), (nr, lanes))

    def residual_sum(rows, c):
        cols = slice(c, c + lanes)
        z = alpha * x_ref[rows, cols] + z_ref[rows, cols]
        if y2_ref is not None:
            z = z + y2_ref[rows, cols]
        z_ref[rows, cols] = z
        return z

    for r in range(0, tm, nr):
        mu_ref[pl.ds(r, nr), :] = inv_d * lane_total(r, residual_sum)

    def centred_sq(rows, c):
        cols = slice(c, c + lanes)
        zc = z_ref[rows, cols] - mu_ref[rows, :]
        z_ref[rows, cols] = zc
        return zc * zc

    for r in range(0, tm, nr):
        rs_ref[pl.ds(r, nr), :] = lax.rsqrt(inv_d * lane_total(r, centred_sq) + LN_EPS)

    for r in range(0, tm, nr):
        rows = pl.ds(r, nr)
        rs = rs_ref[rows, :]
        for c in range(0, d, lanes):
            cols = slice(c, c + lanes)
            o_ref[rows, cols] = z_ref[rows, cols] * rs * g_ref[:, cols] + b_ref[:, cols]


def _causal_conv(buf_ref, r, cols, w, bias, hist, batch):
    k_width = w.shape[0]
    acc = bias
    for k in range(k_width):
        off = hist - (k_width - 1 - k) * batch
        acc = acc + buf_ref[pl.ds(r + off, ROW_BLOCK), cols] * w[k:k + 1, :]
    return acc


def _matmul(lhs_f32, rhs_bf16):
    return lax.dot_general(lhs_f32, rhs_bf16, (((1,), (0,)), ((), ())),
                           preferred_element_type=_F32)


def _project(x_ref, w_ref, cols):
    return _matmul(x_ref[...], w_ref[:, cols])


def _ffn_kernel(x_ref, xn_ref, wup_ref, cw_ref, cb_ref, wdn_ref, g_ref, b_ref, o_ref,
                hbuf, act, zbuf, stats, *out_scratch, alpha, batch, tm, f, batch_major_out):
    hist = (cw_ref.shape[0] - 1) * batch
    fc = COL_BLOCK
    n_pairs = f // fc
    cur = slice(hist, hist + tm)
    step = pl.program_id(0)

    def up_pair(rows_ref, p):
        for c in (p * fc, f + p * fc):
            hbuf[cur, c:c + fc] = _project(rows_ref, wup_ref, slice(c, c + fc))

    @pl.when(pl.program_id(0) == 0)
    def _():
        hbuf[0:hist, :] = jnp.zeros((hist, hbuf.shape[1]), _F32)
        for p in range(FFN_LEAD_PAIRS):
            up_pair(x_ref, p)

    for p in range(FFN_LEAD_PAIRS, n_pairs):
        up_pair(x_ref, p)

    for p in range(n_pairs):
        gcols = slice(p * fc, (p + 1) * fc)
        vcols = slice(f + p * fc, f + (p + 1) * fc)
        wg, bg = 0.5 * cw_ref[:, gcols], 0.5 * cb_ref[:, gcols]
        wv, bv = cw_ref[:, vcols], cb_ref[:, vcols]
        for r in range(0, tm, ROW_BLOCK):
            half = _causal_conv(hbuf, r, gcols, wg, bg, hist, batch)
            val = _causal_conv(hbuf, r, vcols, wv, bv, hist, batch)
            act[pl.ds(r, ROW_BLOCK), gcols] = (half + half * jnp.tanh(half)) * val
        for cols in (gcols, vcols):
            hbuf[0:hist, cols] = hbuf[tm:tm + hist, cols]

    zbuf[...] = _matmul(act[...], wdn_ref[...])

    if batch_major_out:
        rows_out, sems = out_scratch
        slot = step % 2
        ln_out = rows_out.at[slot]
    else:
        ln_out = o_ref
    _residual_ln_store(x_ref, zbuf, g_ref, b_ref, ln_out, stats.at[0], stats.at[1], alpha, tm)

    for p in range(FFN_LEAD_PAIRS):
        up_pair(xn_ref, p)

    if batch_major_out:
        steps_per_tile = tm // batch

        def slab_copy(of_step, of_slot, t):
            return pltpu.make_async_copy(
                rows_out.at[of_slot, pl.ds(t * batch, batch), :],
                o_ref.at[:, of_step * steps_per_tile + t, :], sems.at[of_slot])

        for t in range(steps_per_tile):
            slab_copy(step, slot, t).start()

        @pl.when(step > 0)
        def _():
            for t in range(steps_per_tile):
                slab_copy(step - 1, 1 - slot, t).wait()

        @pl.when(step == pl.num_programs(0) - 1)
        def _():
            for t in range(steps_per_tile):
                slab_copy(step, slot, t).wait()


def _sconv_kernel(x_ref, xn_ref, win_ref, cw_ref, cb_ref, wout_ref, g_ref, b_ref, o_ref,
                  hbuf, cvbuf, gated, zbuf, stats, *, alpha, batch, tm, d):
    hist = (cw_ref.shape[0] - 1) * batch
    dc = COL_BLOCK
    n_chunks = d // dc

    def cv_pair(rows_ref, p):
        for c in (d + p * dc, 2 * d + p * dc):
            hbuf[:, c:c + dc] = _project(rows_ref, win_ref, slice(c, c + dc))

    @pl.when(pl.program_id(0) == 0)
    def _():
        cvbuf[0:hist, :] = jnp.zeros((hist, d), _F32)
        for p in range(SCONV_LEAD_PAIRS):
            cv_pair(x_ref, p)

    for p in range(SCONV_LEAD_PAIRS, n_chunks):
        cv_pair(x_ref, p)
    for p in range(n_chunks):
        hbuf[:, p * dc:(p + 1) * dc] = _project(x_ref, win_ref, slice(p * dc, (p + 1) * dc))

    for p in range(n_chunks):
        c = p * dc
        for r in range(0, tm, ROW_BLOCK):
            rows = pl.ds(r, ROW_BLOCK)
            cvbuf[pl.ds(hist + r, ROW_BLOCK), c:c + dc] = (
                hbuf[rows, d + c:d + c + dc] * hbuf[rows, 2 * d + c:2 * d + c + dc])
    for p in range(n_chunks):
        cols = slice(p * dc, (p + 1) * dc)
        w, bias = cw_ref[:, cols], cb_ref[:, cols]
        for r in range(0, tm, ROW_BLOCK):
            rows = pl.ds(r, ROW_BLOCK)
            u = _causal_conv(cvbuf, r, cols, w, bias, hist, batch)
            gated[rows, cols] = hbuf[rows, cols] * u
        cvbuf[0:hist, cols] = cvbuf[tm:tm + hist, cols]

    zbuf[...] = _matmul(gated[...], wout_ref[...])

    for p in range(SCONV_LEAD_PAIRS):
        cv_pair(xn_ref, p)

    _residual_ln_store(x_ref, zbuf, g_ref, b_ref, o_ref, stats.at[0], stats.at[1], alpha, tm)


def _rglru_kernel(x_ref, xn_ref, win_ref, bin_ref, cw_ref, cb_ref, wg_ref, bg_ref, lam_ref,
                  wout_ref, g_ref, b_ref, o_ref,
                  rbuf, gbr, xr, gates, state, gated, zbuf, zearly, stats,
                  *, alpha, batch, tm, r_width, heads):
    hist = (cw_ref.shape[0] - 1) * batch
    blk = r_width // heads
    ic = COL_BLOCK
    n_chunks = r_width // ic
    cur = slice(hist, hist + tm)

    w = 0.5 * cw_ref[...]
    w_sum = w[0:1, :]
    for k in range(1, w.shape[0]):
        w_sum = w_sum + w[k:k + 1, :]
    conv_bias = 0.5 * cb_ref[...] + bin_ref[:, r_width:2 * r_width] * w_sum

    def r_chunk(rows_ref, j):
        rbuf[cur, j * ic:(j + 1) * ic] = _project(
            rows_ref, win_ref, slice(r_width + j * ic, r_width + (j + 1) * ic))

    def g_chunk(rows_ref, j):
        gbr[:, j * ic:(j + 1) * ic] = _project(rows_ref, win_ref, slice(j * ic, (j + 1) * ic))

    def conv_chunk(j):
        cols = slice(j * ic, (j + 1) * ic)
        wc, bc = w[:, cols], conv_bias[:, cols]
        for r in range(0, tm, ROW_BLOCK):
            rows = pl.ds(r, ROW_BLOCK)
            xr[rows, cols] = _causal_conv(rbuf, r, cols, wc, bc, hist, batch)
        rbuf[0:hist, cols] = rbuf[tm:tm + hist, cols]

    @pl.when(pl.program_id(0) == 0)
    def _():
        rbuf[0:hist, :] = jnp.broadcast_to(-bin_ref[:, r_width:2 * r_width], (hist, r_width))
        state[...] = jnp.zeros(state.shape, _F32)
        for j in range(n_chunks):
            r_chunk(x_ref, j)
        for j in range(n_chunks):
            conv_chunk(j)

    neg_lam = -lam_ref[...]
    softplus = jnp.maximum(neg_lam, 0.0) + jnp.log1p(jnp.exp(-jnp.abs(neg_lam)))
    half_scale = (-0.5 * LRU_C) * softplus

    k_split = RGLRU_EARLY_OUT_HEADS * blk
    for h in range(heads):
        cols = slice(h * blk, (h + 1) * blk)
        gts = gates.at[h % 2]

        gts[...] = _matmul(xr[:, cols], wg_ref[h])
        if h < n_chunks:
            g_chunk(x_ref, h)
        if h == min(RGLRU_EARLY_OUT_HEADS + 1, heads - 1):
            zearly[...] = _matmul(gated[:, 0:k_split], wout_ref[0:k_split, :])

        hs = half_scale[:, cols]
        half_bg = 0.5 * bg_ref[h]
        g_bias = bin_ref[:, cols]
        hstate = state[:, cols]
        for t in range(tm // batch):
            rows = pl.ds(t * batch, batch)
            half_gates = gts[rows, :] + half_bg
            log_a = hs + hs * jnp.tanh(half_gates[:, 0:blk])
            half_xr = xr[rows, cols]
            gated_x = half_xr + half_xr * jnp.tanh(half_gates[:, blk:2 * blk])
            a = jnp.exp(log_a)
            one_minus_a2 = jnp.tanh(log_a) * (-1.0 - a * a)
            mult = one_minus_a2 * lax.rsqrt(jnp.maximum(one_minus_a2, F32_MIN_NORMAL))
            hstate = a * hstate + mult * gated_x
            gated[rows, cols] = hstate * _twice_gelu_tanh(gbr[rows, cols] + g_bias)
        state[:, cols] = hstate

    for j in range(n_chunks - RGLRU_TAIL_R_CHUNKS):
        r_chunk(xn_ref, j)

    zbuf[...] = _matmul(gated[:, k_split:r_width], wout_ref[k_split:r_width, :])

    for j in range(n_chunks - RGLRU_TAIL_R_CHUNKS, n_chunks):
        r_chunk(xn_ref, j)
    for j in range(n_chunks):
        conv_chunk(j)

    _residual_ln_store(x_ref, zbuf, g_ref, b_ref, o_ref, stats.at[0], stats.at[1], alpha, tm,
                       y2_ref=zearly)


def _layer(stacked, *index):
    return stacked, index


def _resident(operand):
    stacked, index = operand
    block = (None,) * len(index) + stacked.shape[len(index):]
    origin = index + (0,) * (stacked.ndim - len(index))
    return pl.BlockSpec(block, lambda s: origin, pipeline_mode=pl.Buffered(1))


def _sublayer_call(body, name, x2, operands, scratch, tm, out_batch=None):
    rows, d = x2.shape
    n_tiles = rows // tm
    cur_spec = pl.BlockSpec((tm, d), lambda s: (s, 0))
    next_spec = pl.BlockSpec((tm, d), lambda s: (jnp.minimum(s + 1, n_tiles - 1), 0))
    if out_batch is None:
        out_shape, out_spec = jax.ShapeDtypeStruct((rows, d), _F32), cur_spec
    else:
        out_shape = jax.ShapeDtypeStruct((out_batch, rows // out_batch, d), _F32)
        out_spec = pl.BlockSpec(memory_space=pl.ANY)
    return pl.pallas_call(
        body,
        out_shape=out_shape,
        grid=(n_tiles,),
        in_specs=[cur_spec, next_spec] + [_resident(op) for op in operands],
        out_specs=out_spec,
        scratch_shapes=scratch,
        compiler_params=pltpu.CompilerParams(
            dimension_semantics=("arbitrary",),
            vmem_limit_bytes=VMEM_LIMIT_BYTES),
        name=name,
    )(x2, x2, *[stacked for stacked, _ in operands])


def _rows(v):
    return v.reshape(v.shape[:-1] + (1, v.shape[-1]))


def _sconv_layer(x2, p, j, ln, *, alpha, batch):
    tm = ROW_TILE
    d = x2.shape[1]
    assert d % COL_BLOCK == 0 and SCONV_LEAD_PAIRS <= d // COL_BLOCK
    hist = (p["conv_w"].shape[1] - 1) * batch
    vm = pltpu.VMEM
    body = functools.partial(_sconv_kernel, alpha=alpha, batch=batch, tm=tm, d=d)
    return _sublayer_call(
        body, "sconv_mixer", x2,
        [_layer(p["w_in"], j), _layer(p["conv_w"], j), _layer(p["conv_b"], j),
         _layer(p["w_out"], j), _layer(ln["g"], *ln["at"]), _layer(ln["b"], *ln["at"])],
        [vm((tm, 3 * d), _F32), vm((hist + tm, d), _F32), vm((tm, d), _F32), vm((tm, d), _F32),
         vm((2, tm, LANES), _F32)],
        tm)


def _rglru_layer(x2, p, j, ln, *, alpha, batch):
    tm = ROW_TILE
    d = x2.shape[1]
    r_width = p["w_out"].shape[1]
    heads = p["w_gate"].shape[1]
    blk = r_width // heads
    assert r_width % COL_BLOCK == 0 and COL_BLOCK % blk == 0
    assert (RGLRU_EARLY_OUT_HEADS * blk) % COL_BLOCK == 0 and RGLRU_EARLY_OUT_HEADS < heads
    hist = (p["conv_w"].shape[1] - 1) * batch
    vm = pltpu.VMEM
    body = functools.partial(_rglru_kernel, alpha=alpha, batch=batch, tm=tm,
                             r_width=r_width, heads=heads)
    wide = vm((tm, r_width), _F32)
    return _sublayer_call(
        body, "rglru_mixer", x2,
        [_layer(p["w_in"], j), _layer(p["b_in"], j), _layer(p["conv_w"], j),
         _layer(p["conv_b"], j), _layer(p["w_gate"], j), _layer(p["b_gate"], j),
         _layer(p["lam"], j), _layer(p["w_out"], j),
         _layer(ln["g"], *ln["at"]), _layer(ln["b"], *ln["at"])],
        [vm((hist + tm, r_width), _F32), wide, wide, vm((2, tm, 2 * blk), _F32),
         vm((batch, r_width), _F32), wide, vm((tm, d), _F32), vm((tm, d), _F32),
         vm((2, tm, LANES), _F32)],
        tm)


def _ffn_layer(x2, p, i, ln, *, alpha, batch, batch_major_out=False):
    tm = ROW_TILE
    d = x2.shape[1]
    f = p["w_down"].shape[1]
    assert f % COL_BLOCK == 0 and FFN_LEAD_PAIRS <= f // COL_BLOCK
    hist = (p["conv_w"].shape[1] - 1) * batch
    vm = pltpu.VMEM
    body = functools.partial(_ffn_kernel, alpha=alpha, batch=batch, tm=tm, f=f,
                             batch_major_out=batch_major_out)
    scratch = [vm((hist + tm, 2 * f), _F32), vm((tm, f), _F32), vm((tm, d), _F32),
               vm((2, tm, LANES), _F32)]
    if batch_major_out:
        scratch += [vm((2, tm, d), _F32), pltpu.SemaphoreType.DMA((2,))]
    return _sublayer_call(
        body, "conv_ffn_out" if batch_major_out else "conv_ffn", x2,
        [_layer(p["w_up"], i), _layer(p["conv_w"], i), _layer(p["conv_b"], i),
         _layer(p["w_down"], i), _layer(ln["g"], *ln["at"]), _layer(ln["b"], *ln["at"])],
        scratch, tm, out_batch=batch if batch_major_out else None)


def kernel(x, sc_w_in, sc_conv_w, sc_conv_b, sc_w_out, lru_w_in, lru_b_in, lru_conv_w, lru_conv_b,
           lru_w_gate, lru_b_gate, lru_lambda, lru_w_out, ffn_w_up, ffn_conv_w, ffn_conv_b,
           ffn_w_down, ln_g, ln_b):
    batch, seq, d = x.shape
    depth = ffn_w_up.shape[0]
    alpha = (2.0 * depth) ** 0.25
    assert ROW_TILE % batch == 0 and (seq * batch) % ROW_TILE == 0 and batch % 8 == 0
    assert ROW_TILE % ROW_BLOCK == 0 and d % LANES == 0

    sconv = dict(w_in=sc_w_in.astype(_BF16), conv_w=sc_conv_w, conv_b=_rows(sc_conv_b),
                 w_out=sc_w_out.astype(_BF16))
    rglru = dict(w_in=lru_w_in.astype(_BF16), b_in=_rows(lru_b_in), conv_w=lru_conv_w,
                 conv_b=_rows(lru_conv_b), w_gate=lru_w_gate.astype(_BF16),
                 b_gate=_rows(lru_b_gate), lam=_rows(lru_lambda),
                 w_out=(0.5 * lru_w_out).astype(_BF16))
    ffn = dict(w_up=ffn_w_up.astype(_BF16), conv_w=ffn_conv_w, conv_b=_rows(ffn_conv_b),
               w_down=ffn_w_down.astype(_BF16))
    ln_g, ln_b = _rows(ln_g), _rows(ln_b)

    x2 = jnp.transpose(x, (1, 0, 2)).reshape(seq * batch, d)
    for i in range(depth):
        j = i // 2
        ln = dict(g=ln_g, b=ln_b, at=(i, 0))
        if i % 2 == 0:
            x2 = _sconv_layer(x2, sconv, j, ln, alpha=alpha, batch=batch)
        else:
            x2 = _rglru_layer(x2, rglru, j, ln, alpha=alpha, batch=batch)
        ln = dict(g=ln_g, b=ln_b, at=(i, 1))
        x2 = _ffn_layer(x2, ffn, i, ln, alpha=alpha, batch=batch,
                        batch_major_out=(i == depth - 1))
    return x2
```

```python
import functools
import math

import jax
import jax.numpy as jnp
from jax import lax
from jax.experimental import pallas as pl
from jax.experimental.pallas import tpu as pltpu

LN_EPS = 1e-5
LRU_C = 8.0
ROW_TILE = 512
ROW_BLOCK = 32
COL_BLOCK = 256
LANES = 128
LN_ROWS = 8
FFN_LEAD_PAIRS = 2
SCONV_LEAD_PAIRS = 2
RGLRU_EARLY_OUT_HEADS = 4
RGLRU_TAIL_R_CHUNKS = 0
VMEM_LIMIT_BYTES = 56 * 1024 * 1024
F32_MIN_NORMAL = 1.1754943508222875e-38
_BF16 = jnp.bfloat16
_F32 = jnp.float32


def _twice_gelu_tanh(z):
    c = math.sqrt(2.0 / math.pi)
    return z + z * jnp.tanh(z * (c + (c * 0.044715) * (z * z)))


def _residual_ln_store(x_ref, z_ref, g_ref, b_ref, o_ref, mu_ref, rs_ref, alpha, tm, y2_ref=None):
    d = z_ref.shape[1]
    lanes = mu_ref.shape[1]
    inv_d = 1.0 / d
    nr = LN_ROWS

    def lane_total(r, fn):
        rows = pl.ds(r, nr)
        acc = fn(rows, 0)
        for c in range(lanes, d, lanes):
            acc = acc + fn(rows, c)
        return jnp.broadcast_to(jnp.sum(acc, axis=-1, keepdims=True), (nr, lanes))

    def residual_sum(rows, c):
        cols = slice(c, c + lanes)
        z = alpha * x_ref[rows, cols] + z_ref[rows, cols]
        if y2_ref is not None:
            z = z + y2_ref[rows, cols]
        z_ref[rows, cols] = z
        return z

    for r in range(0, tm, nr):
        mu_ref[pl.ds(r, nr), :] = inv_d * lane_total(r, residual_sum)

    def centred_sq(rows, c):
        cols = slice(c, c + lanes)
        zc = z_ref[rows, cols] - mu_ref[rows, :]
        z_ref[rows, cols] = zc
        return zc * zc

    for r in range(0, tm, nr):
        rs_ref[pl.ds(r, nr), :] = lax.rsqrt(inv_d * lane_total(r, centred_sq) + LN_EPS)

    for r in range(0, tm, nr):
        rows = pl.ds(r, nr)
        rs = rs_ref[rows, :]
        for c in range(0, d, lanes):
            cols = slice(c, c + lanes)
            o_ref[rows, cols] = z_ref[rows, cols] * rs * g_ref[:, cols] + b_ref[:, cols]


def _causal_conv(buf_ref, r, cols, w, bias, hist, batch):
    k_width = w.shape[0]
    acc = bias
    for k in range(k_width):
        off = hist - (k_width - 1 - k) * batch
        acc = acc + buf_ref[pl.ds(r + off, ROW_BLOCK), cols] * w[k:k + 1, :]
    return acc


def _matmul(lhs_f32, rhs_bf16):
    return lax.dot_general(lhs_f32, rhs_bf16, (((1,), (0,)), ((), ())),
                           preferred_element_type=_F32)


def _project(x_ref, w_ref, cols):
    return _matmul(x_ref[...], w_ref[:, cols])


def _ffn_kernel(x_ref, xn_ref, wup_ref, cw_ref, cb_ref, wdn_ref, g_ref, b_ref, o_ref,
                hbuf, act, zbuf, stats, *out_scratch, alpha, batch, tm, f, batch_major_out):
    hist = (cw_ref.shape[0] - 1) * batch
    fc = COL_BLOCK
    n_pairs = f // fc
    cur = slice(hist, hist + tm)
    step = pl.program_id(0)

    def up_pair(rows_ref, p):
        for c in (p * fc, f + p * fc):
            hbuf[cur, c:c + fc] = _project(rows_ref, wup_ref, slice(c, c + fc))

    @pl.when(pl.program_id(0) == 0)
    def _():
        hbuf[0:hist, :] = jnp.zeros((hist, hbuf.shape[1]), _F32)
        for p in range(FFN_LEAD_PAIRS):
            up_pair(x_ref, p)

    for p in range(FFN_LEAD_PAIRS, n_pairs):
        up_pair(x_ref, p)

    for p in range(n_pairs):
        gcols = slice(p * fc, (p + 1) * fc)
        vcols = slice(f + p * fc, f + (p + 1) * fc)
        wg, bg = 0.5 * cw_ref[:, gcols], 0.5 * cb_ref[:, gcols]
        wv, bv = cw_ref[:, vcols], cb_ref[:, vcols]
        for r in range(0, tm, ROW_BLOCK):
            half = _causal_conv(hbuf, r, gcols, wg, bg, hist, batch)
            val = _causal_conv(hbuf, r, vcols, wv, bv, hist, batch)
            act[pl.ds(r, ROW_BLOCK), gcols] = (half + half * jnp.tanh(half)) * val
        for cols in (gcols, vcols):
            hbuf[0:hist, cols] = hbuf[tm:tm + hist, cols]

    zbuf[...] = _matmul(act[...], wdn_ref[...])

    if batch_major_out:
        rows_out, sems = out_scratch
        slot = step % 2
        ln_out = rows_out.at[slot]
    else:
        ln_out = o_ref
    _residual_ln_store(x_ref, zbuf, g_ref, b_ref, ln_out, stats.at[0], stats.at[1], alpha, tm)

    for p in range(FFN_LEAD_PAIRS):
        up_pair(xn_ref, p)

    if batch_major_out:
        steps_per_tile = tm // batch

        def slab_copy(of_step, of_slot, t):
            return pltpu.make_async_copy(
                rows_out.at[of_slot, pl.ds(t * batch, batch), :],
                o_ref.at[:, of_step * steps_per_tile + t, :], sems.at[of_slot])

        for t in range(steps_per_tile):
            slab_copy(step, slot, t).start()

        @pl.when(step > 0)
        def _():
            for t in range(steps_per_tile):
                slab_copy(step - 1, 1 - slot, t).wait()

        @pl.when(step == pl.num_programs(0) - 1)
        def _():
            for t in range(steps_per_tile):
                slab_copy(step, slot, t).wait()


def _sconv_kernel(x_ref, xn_ref, win_ref, cw_ref, cb_ref, wout_ref, g_ref, b_ref, o_ref,
                  hbuf, cvbuf, gated, zbuf, stats, *, alpha, batch, tm, d):
    hist = (cw_ref.shape[0] - 1) * batch
    dc = COL_BLOCK
    n_chunks = d // dc

    def cv_pair(rows_ref, p):
        for c in (d + p * dc, 2 * d + p * dc):
            hbuf[:, c:c + dc] = _project(rows_ref, win_ref, slice(c, c + dc))

    @pl.when(pl.program_id(0) == 0)
    def _():
        cvbuf[0:hist, :] = jnp.zeros((hist, d), _F32)
        for p in range(SCONV_LEAD_PAIRS):
            cv_pair(x_ref, p)

    for p in range(SCONV_LEAD_PAIRS, n_chunks):
        cv_pair(x_ref, p)
    for p in range(n_chunks):
        hbuf[:, p * dc:(p + 1) * dc] = _project(x_ref, win_ref, slice(p * dc, (p + 1) * dc))

    for p in range(n_chunks):
        c = p * dc
        for r in range(0, tm, ROW_BLOCK):
            rows = pl.ds(r, ROW_BLOCK)
            cvbuf[pl.ds(hist + r, ROW_BLOCK), c:c + dc] = (
                hbuf[rows, d + c:d + c + dc] * hbuf[rows, 2 * d + c:2 * d + c + dc])
    for p in range(n_chunks):
        cols = slice(p * dc, (p + 1) * dc)
        w, bias = cw_ref[:, cols], cb_ref[:, cols]
        for r in range(0, tm, ROW_BLOCK):
            rows = pl.ds(r, ROW_BLOCK)
            u = _causal_conv(cvbuf, r, cols, w, bias, hist, batch)
            gated[rows, cols] = hbuf[rows, cols] * u
        cvbuf[0:hist, cols] = cvbuf[tm:tm + hist, cols]

    zbuf[...] = _matmul(gated[...], wout_ref[...])

    for p in range(SCONV_LEAD_PAIRS):
        cv_pair(xn_ref, p)

    _residual_ln_store(x_ref, zbuf, g_ref, b_ref, o_ref, stats.at[0], stats.at[1], alpha, tm)


def _sconv_gather_kernel(x_hbm, *refs, n_tiles, batch, tm, **static):
    *body_refs, ring, sems = refs
    step = pl.program_id(0)
    steps_per_tile = tm // batch

    def slab_copy(tile, slot, t):
        return pltpu.make_async_copy(
            x_hbm.at[:, tile * steps_per_tile + t, :],
            ring.at[slot, pl.ds(t * batch, batch), :], sems.at[slot])

    def fetch(tile, slot):
        for t in range(steps_per_tile):
            slab_copy(tile, slot, t).start()

    def wait(tile, slot):
        for t in range(steps_per_tile):
            slab_copy(tile, slot, t).wait()

    @pl.when(step == 0)
    def _():
        fetch(0, 0)
        fetch(min(1, n_tiles - 1), 1)
        wait(0, 0)
        wait(min(1, n_tiles - 1), 1)

    ahead_tile = jnp.minimum(step + 2, n_tiles - 1)
    ahead_slot = lax.rem(step + 2, 3)
    fetch(ahead_tile, ahead_slot)

    _sconv_kernel(ring.at[lax.rem(step, 3)], ring.at[lax.rem(step + 1, 3)], *body_refs,
                  batch=batch, tm=tm, **static)

    wait(ahead_tile, ahead_slot)


def _rglru_kernel(x_ref, xn_ref, win_ref, bin_ref, cw_ref, cb_ref, wg_ref, bg_ref, lam_ref,
                  wout_ref, g_ref, b_ref, o_ref,
                  rbuf, gbr, xr, gates, state, gated, zbuf, zearly, stats,
                  *, alpha, batch, tm, r_width, heads):
    hist = (cw_ref.shape[0] - 1) * batch
    blk = r_width // heads
    ic = COL_BLOCK
    n_chunks = r_width // ic
    cur = slice(hist, hist + tm)

    w = 0.5 * cw_ref[...]
    w_sum = w[0:1, :]
    for k in range(1, w.shape[0]):
        w_sum = w_sum + w[k:k + 1, :]
    conv_bias = 0.5 * cb_ref[...] + bin_ref[:, r_width:2 * r_width] * w_sum

    def r_chunk(rows_ref, j):
        rbuf[cur, j * ic:(j + 1) * ic] = _project(
            rows_ref, win_ref, slice(r_width + j * ic, r_width + (j + 1) * ic))

    def g_chunk(rows_ref, j):
        gbr[:, j * ic:(j + 1) * ic] = _project(rows_ref, win_ref, slice(j * ic, (j + 1) * ic))

    def conv_chunk(j):
        cols = slice(j * ic, (j + 1) * ic)
        wc, bc = w[:, cols], conv_bias[:, cols]
        for r in range(0, tm, ROW_BLOCK):
            rows = pl.ds(r, ROW_BLOCK)
            xr[rows, cols] = _causal_conv(rbuf, r, cols, wc, bc, hist, batch)
        rbuf[0:hist, cols] = rbuf[tm:tm + hist, cols]

    @pl.when(pl.program_id(0) == 0)
    def _():
        rbuf[0:hist, :] = jnp.broadcast_to(-bin_ref[:, r_width:2 * r_width], (hist, r_width))
        state[...] = jnp.zeros(state.shape, _F32)
        for j in range(n_chunks):
            r_chunk(x_ref, j)
        for j in range(n_chunks):
            conv_chunk(j)

    neg_lam = -lam_ref[...]
    softplus = jnp.maximum(neg_lam, 0.0) + jnp.log1p(jnp.exp(-jnp.abs(neg_lam)))
    half_scale = (-0.5 * LRU_C) * softplus

    k_split = RGLRU_EARLY_OUT_HEADS * blk
    for h in range(heads):
        cols = slice(h * blk, (h + 1) * blk)
        gts = gates.at[h % 2]

        gts[...] = _matmul(xr[:, cols], wg_ref[h])
        if h < n_chunks:
            g_chunk(x_ref, h)
        if h == min(RGLRU_EARLY_OUT_HEADS + 1, heads - 1):
            zearly[...] = _matmul(gated[:, 0:k_split], wout_ref[0:k_split, :])

        hs = half_scale[:, cols]
        half_bg = 0.5 * bg_ref[h]
        g_bias = bin_ref[:, cols]
        hstate = state[:, cols]
        for t in range(tm // batch):
            rows = pl.ds(t * batch, batch)
            half_gates = gts[rows, :] + half_bg
            log_a = hs + hs * jnp.tanh(half_gates[:, 0:blk])
            half_xr = xr[rows, cols]
            gated_x = half_xr + half_xr * jnp.tanh(half_gates[:, blk:2 * blk])
            a = jnp.exp(log_a)
            one_minus_a2 = jnp.tanh(log_a) * (-1.0 - a * a)
            mult = one_minus_a2 * lax.rsqrt(jnp.maximum(one_minus_a2, F32_MIN_NORMAL))
            hstate = a * hstate + mult * gated_x
            gated[rows, cols] = hstate * _twice_gelu_tanh(gbr[rows, cols] + g_bias)
        state[:, cols] = hstate

    for j in range(n_chunks - RGLRU_TAIL_R_CHUNKS):
        r_chunk(xn_ref, j)

    zbuf[...] = _matmul(gated[:, k_split:r_width], wout_ref[k_split:r_width, :])

    for j in range(n_chunks - RGLRU_TAIL_R_CHUNKS, n_chunks):
        r_chunk(xn_ref, j)
    for j in range(n_chunks):
        conv_chunk(j)

    _residual_ln_store(x_ref, zbuf, g_ref, b_ref, o_ref, stats.at[0], stats.at[1], alpha, tm,
                       y2_ref=zearly)


def _layer(stacked, *index):
    return stacked, index


def _resident(operand):
    stacked, index = operand
    block = (None,) * len(index) + stacked.shape[len(index):]
    origin = index + (0,) * (stacked.ndim - len(index))
    return pl.BlockSpec(block, lambda s: origin, pipeline_mode=pl.Buffered(1))


def _sublayer_call(body, name, x2, operands, scratch, tm, out_batch=None, x_in_hbm=False):
    d = x2.shape[-1]
    rows = x2.size // d
    n_tiles = rows // tm
    cur_spec = pl.BlockSpec((tm, d), lambda s: (s, 0))
    next_spec = pl.BlockSpec((tm, d), lambda s: (jnp.minimum(s + 1, n_tiles - 1), 0))
    if x_in_hbm:
        row_specs, row_args = [pl.BlockSpec(memory_space=pl.ANY)], [x2]
    else:
        row_specs, row_args = [cur_spec, next_spec], [x2, x2]
    if out_batch is None:
        out_shape, out_spec = jax.ShapeDtypeStruct((rows, d), _F32), cur_spec
    else:
        out_shape = jax.ShapeDtypeStruct((out_batch, rows // out_batch, d), _F32)
        out_spec = pl.BlockSpec(memory_space=pl.ANY)
    return pl.pallas_call(
        body,
        out_shape=out_shape,
        grid=(n_tiles,),
        in_specs=row_specs + [_resident(op) for op in operands],
        out_specs=out_spec,
        scratch_shapes=scratch,
        compiler_params=pltpu.CompilerParams(
            dimension_semantics=("arbitrary",),
            vmem_limit_bytes=VMEM_LIMIT_BYTES),
        name=name,
    )(*row_args, *[stacked for stacked, _ in operands])


def _rows(v):
    return v.reshape(v.shape[:-1] + (1, v.shape[-1]))


def _sconv_layer(x2, p, j, ln, *, alpha, batch):
    tm = ROW_TILE
    d = x2.shape[-1]
    assert d % COL_BLOCK == 0 and SCONV_LEAD_PAIRS <= d // COL_BLOCK
    hist = (p["conv_w"].shape[1] - 1) * batch
    vm = pltpu.VMEM
    scratch = [vm((tm, 3 * d), _F32), vm((hist + tm, d), _F32), vm((tm, d), _F32),
               vm((tm, d), _F32), vm((2, tm, LANES), _F32)]
    batch_major_in = x2.ndim == 3
    if batch_major_in:
        body = functools.partial(_sconv_gather_kernel, n_tiles=x2.size // (d * tm),
                                 alpha=alpha, batch=batch, tm=tm, d=d)
        scratch += [vm((3, tm, d), _F32), pltpu.SemaphoreType.DMA((3,))]
    else:
        body = functools.partial(_sconv_kernel, alpha=alpha, batch=batch, tm=tm, d=d)
    return _sublayer_call(
        body, "sconv_mixer_in" if batch_major_in else "sconv_mixer", x2,
        [_layer(p["w_in"], j), _layer(p["conv_w"], j), _layer(p["conv_b"], j),
         _layer(p["w_out"], j), _layer(ln["g"], *ln["at"]), _layer(ln["b"], *ln["at"])],
        scratch, tm, x_in_hbm=batch_major_in)


def _rglru_layer(x2, p, j, ln, *, alpha, batch):
    tm = ROW_TILE
    d = x2.shape[1]
    r_width = p["w_out"].shape[1]
    heads = p["w_gate"].shape[1]
    blk = r_width // heads
    assert r_width % COL_BLOCK == 0 and COL_BLOCK % blk == 0
    assert (RGLRU_EARLY_OUT_HEADS * blk) % COL_BLOCK == 0 and RGLRU_EARLY_OUT_HEADS < heads
    hist = (p["conv_w"].shape[1] - 1) * batch
    vm = pltpu.VMEM
    body = functools.partial(_rglru_kernel, alpha=alpha, batch=batch, tm=tm,
                             r_width=r_width, heads=heads)
    wide = vm((tm, r_width), _F32)
    return _sublayer_call(
        body, "rglru_mixer", x2,
        [_layer(p["w_in"], j), _layer(p["b_in"], j), _layer(p["conv_w"], j),
         _layer(p["conv_b"], j), _layer(p["w_gate"], j), _layer(p["b_gate"], j),
         _layer(p["lam"], j), _layer(p["w_out"], j),
         _layer(ln["g"], *ln["at"]), _layer(ln["b"], *ln["at"])],
        [vm((hist + tm, r_width), _F32), wide, wide, vm((2, tm, 2 * blk), _F32),
         vm((batch, r_width), _F32), wide, vm((tm, d), _F32), vm((tm, d), _F32),
         vm((2, tm, LANES), _F32)],
        tm)


def _ffn_layer(x2, p, i, ln, *, alpha, batch, batch_major_out=False):
    tm = ROW_TILE
    d = x2.shape[1]
    f = p["w_down"].shape[1]
    assert f % COL_BLOCK == 0 and FFN_LEAD_PAIRS <= f // COL_BLOCK
    hist = (p["conv_w"].shape[1] - 1) * batch
    vm = pltpu.VMEM
    body = functools.partial(_ffn_kernel, alpha=alpha, batch=batch, tm=tm, f=f,
                             batch_major_out=batch_major_out)
    scratch = [vm((hist + tm, 2 * f), _F32), vm((tm, f), _F32), vm((tm, d), _F32),
               vm((2, tm, LANES), _F32)]
    if batch_major_out:
        scratch += [vm((2, tm, d), _F32), pltpu.SemaphoreType.DMA((2,))]
    return _sublayer_call(
        body, "conv_ffn_out" if batch_major_out else "conv_ffn", x2,
        [_layer(p["w_up"], i), _layer(p["conv_w"], i), _layer(p["conv_b"], i),
         _layer(p["w_down"], i), _layer(ln["g"], *ln["at"]), _layer(ln["b"], *ln["at"])],
        scratch, tm, out_batch=batch if batch_major_out else None)


def kernel(x, sc_w_in, sc_conv_w, sc_conv_b, sc_w_out, lru_w_in, lru_b_in, lru_conv_w, lru_conv_b,
           lru_w_gate, lru_b_gate, lru_lambda, lru_w_out, ffn_w_up, ffn_conv_w, ffn_conv_b,
           ffn_w_down, ln_g, ln_b):
    batch, seq, d = x.shape
    depth = ffn_w_up.shape[0]
    alpha = (2.0 * depth) ** 0.25
    assert ROW_TILE % batch == 0 and (seq * batch) % ROW_TILE == 0 and batch % 8 == 0
    assert ROW_TILE % ROW_BLOCK == 0 and d % LANES == 0

    sconv = dict(w_in=sc_w_in.astype(_BF16), conv_w=sc_conv_w, conv_b=_rows(sc_conv_b),
                 w_out=sc_w_out.astype(_BF16))
    rglru = dict(w_in=lru_w_in.astype(_BF16), b_in=_rows(lru_b_in), conv_w=lru_conv_w,
                 conv_b=_rows(lru_conv_b), w_gate=lru_w_gate.astype(_BF16),
                 b_gate=_rows(lru_b_gate), lam=_rows(lru_lambda),
                 w_out=(0.5 * lru_w_out).astype(_BF16))
    ffn = dict(w_up=ffn_w_up.astype(_BF16), conv_w=ffn_conv_w, conv_b=_rows(ffn_conv_b),
               w_down=ffn_w_down.astype(_BF16))
    ln_g, ln_b = _rows(ln_g), _rows(ln_b)

    assert depth >= 1 and x.dtype == _F32
    x2 = x
    for i in range(depth):
        j = i // 2
        ln = dict(g=ln_g, b=ln_b, at=(i, 0))
        if i % 2 == 0:
            x2 = _sconv_layer(x2, sconv, j, ln, alpha=alpha, batch=batch)
        else:
            x2 = _rglru_layer(x2, rglru, j, ln, alpha=alpha, batch=batch)
        ln = dict(g=ln_g, b=ln_b, at=(i, 1))
        x2 = _ffn_layer(x2, ffn, i, ln, alpha=alpha, batch=batch,
                        batch_major_out=(i == depth - 1))
    return x2
```

```python
import functools
import math

import jax
import jax.numpy as jnp
from jax import lax
from jax.experimental import pallas as pl
from jax.experimental.pallas import tpu as pltpu

LN_EPS = 1e-5
LRU_C = 8.0
ROW_TILE = 512
ROW_BLOCK = 32
COL_BLOCK = 256
LANES = 128
LN_ROWS = 8
FFN_LEAD_PAIRS = 2
SCONV_LEAD_PAIRS = 2
RGLRU_EARLY_OUT_HEADS = 4
RGLRU_TAIL_R_CHUNKS = 0
VMEM_LIMIT_BYTES = 56 * 1024 * 1024
F32_MIN_NORMAL = 1.1754943508222875e-38
_BF16 = jnp.bfloat16
_F32 = jnp.float32


def _twice_gelu_tanh(z):
    c = math.sqrt(2.0 / math.pi)
    return z + z * jnp.tanh(z * (c + (c * 0.044715) * (z * z)))


def _residual_ln_store(x_ref, z_ref, g_ref, b_ref, o_ref, mu_ref, rs_ref, alpha, tm, y2_ref=None):
    d = z_ref.shape[1]
    lanes = mu_ref.shape[1]
    inv_d = 1.0 / d
    nr = LN_ROWS

    def lane_total(r, fn):
        rows = pl.ds(r, nr)
        acc = fn(rows, 0)
        for c in range(lanes, d, lanes):
            acc = acc + fn(rows, c)
        return jnp.broadcast_to(jnp.sum(acc, axis=-1, keepdims=True), (nr, lanes))

    def residual_sum(rows, c):
        cols = slice(c, c + lanes)
        z = alpha * x_ref[rows, cols] + z_ref[rows, cols]
        if y2_ref is not None:
            z = z + y2_ref[rows, cols]
        z_ref[rows, cols] = z
        return z

    for r in range(0, tm, nr):
        mu_ref[pl.ds(r, nr), :] = inv_d * lane_total(r, residual_sum)

    def centred_sq(rows, c):
        cols = slice(c, c + lanes)
        zc = z_ref[rows, cols] - mu_ref[rows, :]
        z_ref[rows, cols] = zc
        return zc * zc

    for r in range(0, tm, nr):
        rs_ref[pl.ds(r, nr), :] = lax.rsqrt(inv_d * lane_total(r, centred_sq) + LN_EPS)

    for r in range(0, tm, nr):
        rows = pl.ds(r, nr)
        rs = rs_ref[rows, :]
        for c in range(0, d, lanes):
            cols = slice(c, c + lanes)
            o_ref[rows, cols] = z_ref[rows, cols] * rs * g_ref[:, cols] + b_ref[:, cols]


def _layer_norm_store(z_ref, g_ref, b_ref, o_ref, tm):
    g = g_ref[...]
    b = b_ref[...]
    for r in range(0, tm, LN_ROWS):
        rows = pl.ds(r, LN_ROWS)
        z = z_ref[rows, :]
        zc = z - jnp.mean(z, axis=-1, keepdims=True)
        var = jnp.mean(zc * zc, axis=-1, keepdims=True)
        o_ref[rows, :] = zc * lax.rsqrt(var + LN_EPS) * g + b


def _residual_store(x_ref, y_ref, o_ref, alpha, tm, y2_ref=None):
    for r in range(0, tm, ROW_BLOCK):
        rows = pl.ds(r, ROW_BLOCK)
        z = alpha * x_ref[rows, :] + y_ref[rows, :]
        if y2_ref is not None:
            z = z + y2_ref[rows, :]
        o_ref[rows, :] = z


def _causal_conv(buf_ref, r, cols, w, bias, hist, batch):
    k_width = w.shape[0]
    acc = bias
    for k in range(k_width):
        off = hist - (k_width - 1 - k) * batch
        acc = acc + buf_ref[pl.ds(r + off, ROW_BLOCK), cols] * w[k:k + 1, :]
    return acc


def _matmul(lhs_f32, rhs_bf16):
    return lax.dot_general(lhs_f32, rhs_bf16, (((1,), (0,)), ((), ())),
                           preferred_element_type=_F32)


def _project(x_ref, w_ref, cols):
    return _matmul(x_ref[...], w_ref[:, cols])


def _ffn_kernel(z_ref, zn_ref, gin_ref, bin_ref, wup_ref, cw_ref, cb_ref, wdn_ref, g_ref, b_ref,
                o_ref, hbuf, act, zbuf, stats, x_ref, xn_ref, *out_scratch,
                alpha, batch, tm, f, batch_major_out):
    hist = (cw_ref.shape[0] - 1) * batch
    fc = COL_BLOCK
    n_pairs = f // fc
    cur = slice(hist, hist + tm)
    step = pl.program_id(0)

    def up_pair(rows_ref, p):
        for c in (p * fc, f + p * fc):
            hbuf[cur, c:c + fc] = _project(rows_ref, wup_ref, slice(c, c + fc))

    @pl.when(pl.program_id(0) == 0)
    def _():
        hbuf[0:hist, :] = jnp.zeros((hist, hbuf.shape[1]), _F32)
        _layer_norm_store(z_ref, gin_ref, bin_ref, xn_ref, tm)
        for p in range(FFN_LEAD_PAIRS):
            up_pair(xn_ref, p)

    x_ref[...] = xn_ref[...]
    for p in range(FFN_LEAD_PAIRS, n_pairs):
        up_pair(x_ref, p)

    for p in range(n_pairs):
        gcols = slice(p * fc, (p + 1) * fc)
        vcols = slice(f + p * fc, f + (p + 1) * fc)
        wg, bg = 0.5 * cw_ref[:, gcols], 0.5 * cb_ref[:, gcols]
        wv, bv = cw_ref[:, vcols], cb_ref[:, vcols]
        for r in range(0, tm, ROW_BLOCK):
            half = _causal_conv(hbuf, r, gcols, wg, bg, hist, batch)
            val = _causal_conv(hbuf, r, vcols, wv, bv, hist, batch)
            act[pl.ds(r, ROW_BLOCK), gcols] = (half + half * jnp.tanh(half)) * val
        for cols in (gcols, vcols):
            hbuf[0:hist, cols] = hbuf[tm:tm + hist, cols]

    zbuf[...] = _matmul(act[...], wdn_ref[...])

    if batch_major_out:
        rows_out, sems = out_scratch
        slot = step % 2
        ln_out = rows_out.at[slot]
    else:
        ln_out = o_ref
    _residual_ln_store(x_ref, zbuf, g_ref, b_ref, ln_out, stats.at[0], stats.at[1], alpha, tm)

    _layer_norm_store(zn_ref, gin_ref, bin_ref, xn_ref, tm)
    for p in range(FFN_LEAD_PAIRS):
        up_pair(xn_ref, p)

    if batch_major_out:
        steps_per_tile = tm // batch

        def slab_copy(of_step, of_slot, t):
            return pltpu.make_async_copy(
                rows_out.at[of_slot, pl.ds(t * batch, batch), :],
                o_ref.at[:, of_step * steps_per_tile + t, :], sems.at[of_slot])

        for t in range(steps_per_tile):
            slab_copy(step, slot, t).start()

        @pl.when(step > 0)
        def _():
            for t in range(steps_per_tile):
                slab_copy(step - 1, 1 - slot, t).wait()

        @pl.when(step == pl.num_programs(0) - 1)
        def _():
            for t in range(steps_per_tile):
                slab_copy(step, slot, t).wait()


def _sconv_kernel(x_ref, xn_ref, win_ref, cw_ref, cb_ref, wout_ref, o_ref,
                  hbuf, cvbuf, gated, zbuf, *, alpha, batch, tm, d):
    hist = (cw_ref.shape[0] - 1) * batch
    dc = COL_BLOCK
    n_chunks = d // dc

    def cv_pair(rows_ref, p):
        for c in (d + p * dc, 2 * d + p * dc):
            hbuf[:, c:c + dc] = _project(rows_ref, win_ref, slice(c, c + dc))

    @pl.when(pl.program_id(0) == 0)
    def _():
        cvbuf[0:hist, :] = jnp.zeros((hist, d), _F32)
        for p in range(SCONV_LEAD_PAIRS):
            cv_pair(x_ref, p)

    for p in range(SCONV_LEAD_PAIRS, n_chunks):
        cv_pair(x_ref, p)
    for p in range(n_chunks):
        hbuf[:, p * dc:(p + 1) * dc] = _project(x_ref, win_ref, slice(p * dc, (p + 1) * dc))

    for p in range(n_chunks):
        c = p * dc
        for r in range(0, tm, ROW_BLOCK):
            rows = pl.ds(r, ROW_BLOCK)
            cvbuf[pl.ds(hist + r, ROW_BLOCK), c:c + dc] = (
                hbuf[rows, d + c:d + c + dc] * hbuf[rows, 2 * d + c:2 * d + c + dc])
    for p in range(n_chunks):
        cols = slice(p * dc, (p + 1) * dc)
        w, bias = cw_ref[:, cols], cb_ref[:, cols]
        for r in range(0, tm, ROW_BLOCK):
            rows = pl.ds(r, ROW_BLOCK)
            u = _causal_conv(cvbuf, r, cols, w, bias, hist, batch)
            gated[rows, cols] = hbuf[rows, cols] * u
        cvbuf[0:hist, cols] = cvbuf[tm:tm + hist, cols]

    zbuf[...] = _matmul(gated[...], wout_ref[...])

    for p in range(SCONV_LEAD_PAIRS):
        cv_pair(xn_ref, p)

    _residual_store(x_ref, zbuf, o_ref, alpha, tm)


def _sconv_gather_kernel(x_hbm, *refs, n_tiles, batch, tm, **static):
    *body_refs, ring, sems = refs
    step = pl.program_id(0)
    steps_per_tile = tm // batch

    def slab_copy(tile, slot, t):
        return pltpu.make_async_copy(
            x_hbm.at[:, tile * steps_per_tile + t, :],
            ring.at[slot, pl.ds(t * batch, batch), :], sems.at[slot])

    def fetch(tile, slot):
        for t in range(steps_per_tile):
            slab_copy(tile, slot, t).start()

    def wait(tile, slot):
        for t in range(steps_per_tile):
            slab_copy(tile, slot, t).wait()

    @pl.when(step == 0)
    def _():
        fetch(0, 0)
        fetch(min(1, n_tiles - 1), 1)
        wait(0, 0)
        wait(min(1, n_tiles - 1), 1)

    ahead_tile = jnp.minimum(step + 2, n_tiles - 1)
    ahead_slot = lax.rem(step + 2, 3)
    fetch(ahead_tile, ahead_slot)

    _sconv_kernel(ring.at[lax.rem(step, 3)], ring.at[lax.rem(step + 1, 3)], *body_refs,
                  batch=batch, tm=tm, **static)

    wait(ahead_tile, ahead_slot)


def _rglru_kernel(x_ref, xn_ref, win_ref, bin_ref, cw_ref, cb_ref, wg_ref, bg_ref, lam_ref,
                  wout_ref, o_ref,
                  rbuf, gbr, xr, gates, state, gated, zbuf, zearly,
                  *, alpha, batch, tm, r_width, heads):
    hist = (cw_ref.shape[0] - 1) * batch
    blk = r_width // heads
    ic = COL_BLOCK
    n_chunks = r_width // ic
    cur = slice(hist, hist + tm)

    w = 0.5 * cw_ref[...]
    w_sum = w[0:1, :]
    for k in range(1, w.shape[0]):
        w_sum = w_sum + w[k:k + 1, :]
    conv_bias = 0.5 * cb_ref[...] + bin_ref[:, r_width:2 * r_width] * w_sum

    def r_chunk(rows_ref, j):
        rbuf[cur, j * ic:(j + 1) * ic] = _project(
            rows_ref, win_ref, slice(r_width + j * ic, r_width + (j + 1) * ic))

    def g_chunk(rows_ref, j):
        gbr[:, j * ic:(j + 1) * ic] = _project(rows_ref, win_ref, slice(j * ic, (j + 1) * ic))

    def conv_chunk(j):
        cols = slice(j * ic, (j + 1) * ic)
        wc, bc = w[:, cols], conv_bias[:, cols]
        for r in range(0, tm, ROW_BLOCK):
            rows = pl.ds(r, ROW_BLOCK)
            xr[rows, cols] = _causal_conv(rbuf, r, cols, wc, bc, hist, batch)
        rbuf[0:hist, cols] = rbuf[tm:tm + hist, cols]

    @pl.when(pl.program_id(0) == 0)
    def _():
        rbuf[0:hist, :] = jnp.broadcast_to(-bin_ref[:, r_width:2 * r_width], (hist, r_width))
        state[...] = jnp.zeros(state.shape, _F32)
        for j in range(n_chunks):
            r_chunk(x_ref, j)
        for j in range(n_chunks):
            conv_chunk(j)

    neg_lam = -lam_ref[...]
    softplus = jnp.maximum(neg_lam, 0.0) + jnp.log1p(jnp.exp(-jnp.abs(neg_lam)))
    half_scale = (-0.5 * LRU_C) * softplus

    k_split = RGLRU_EARLY_OUT_HEADS * blk
    for h in range(heads):
        cols = slice(h * blk, (h + 1) * blk)
        gts = gates.at[h % 2]

        gts[...] = _matmul(xr[:, cols], wg_ref[h])
        if h < n_chunks:
            g_chunk(x_ref, h)
        if h == min(RGLRU_EARLY_OUT_HEADS + 1, heads - 1):
            zearly[...] = _matmul(gated[:, 0:k_split], wout_ref[0:k_split, :])

        hs = half_scale[:, cols]
        half_bg = 0.5 * bg_ref[h]
        g_bias = bin_ref[:, cols]
        hstate = state[:, cols]
        for t in range(tm // batch):
            rows = pl.ds(t * batch, batch)
            half_gates = gts[rows, :] + half_bg
            log_a = hs + hs * jnp.tanh(half_gates[:, 0:blk])
            half_xr = xr[rows, cols]
            gated_x = half_xr + half_xr * jnp.tanh(half_gates[:, blk:2 * blk])
            a = jnp.exp(log_a)
            one_minus_a2 = jnp.tanh(log_a) * (-1.0 - a * a)
            mult = one_minus_a2 * lax.rsqrt(jnp.maximum(one_minus_a2, F32_MIN_NORMAL))
            hstate = a * hstate + mult * gated_x
            gated[rows, cols] = hstate * _twice_gelu_tanh(gbr[rows, cols] + g_bias)
        state[:, cols] = hstate

    for j in range(n_chunks - RGLRU_TAIL_R_CHUNKS):
        r_chunk(xn_ref, j)

    zbuf[...] = _matmul(gated[:, k_split:r_width], wout_ref[k_split:r_width, :])

    for j in range(n_chunks - RGLRU_TAIL_R_CHUNKS, n_chunks):
        r_chunk(xn_ref, j)
    for j in range(n_chunks):
        conv_chunk(j)

    _residual_store(x_ref, zbuf, o_ref, alpha, tm, y2_ref=zearly)


def _layer(stacked, *index):
    return stacked, index


def _resident(operand):
    stacked, index = operand
    block = (None,) * len(index) + stacked.shape[len(index):]
    origin = index + (0,) * (stacked.ndim - len(index))
    return pl.BlockSpec(block, lambda s: origin, pipeline_mode=pl.Buffered(1))


def _sublayer_call(body, name, x2, operands, scratch, tm, out_batch=None, x_in_hbm=False):
    d = x2.shape[-1]
    rows = x2.size // d
    n_tiles = rows // tm
    cur_spec = pl.BlockSpec((tm, d), lambda s: (s, 0))
    next_spec = pl.BlockSpec((tm, d), lambda s: (jnp.minimum(s + 1, n_tiles - 1), 0))
    if x_in_hbm:
        row_specs, row_args = [pl.BlockSpec(memory_space=pl.ANY)], [x2]
    else:
        row_specs, row_args = [cur_spec, next_spec], [x2, x2]
    if out_batch is None:
        out_shape, out_spec = jax.ShapeDtypeStruct((rows, d), _F32), cur_spec
    else:
        out_shape = jax.ShapeDtypeStruct((out_batch, rows // out_batch, d), _F32)
        out_spec = pl.BlockSpec(memory_space=pl.ANY)
    return pl.pallas_call(
        body,
        out_shape=out_shape,
        grid=(n_tiles,),
        in_specs=row_specs + [_resident(op) for op in operands],
        out_specs=out_spec,
        scratch_shapes=scratch,
        compiler_params=pltpu.CompilerParams(
            dimension_semantics=("arbitrary",),
            vmem_limit_bytes=VMEM_LIMIT_BYTES),
        name=name,
    )(*row_args, *[stacked for stacked, _ in operands])


def _rows(v):
    return v.reshape(v.shape[:-1] + (1, v.shape[-1]))


def _sconv_layer(x2, p, j, *, alpha, batch):
    tm = ROW_TILE
    d = x2.shape[-1]
    assert d % COL_BLOCK == 0 and SCONV_LEAD_PAIRS <= d // COL_BLOCK
    hist = (p["conv_w"].shape[1] - 1) * batch
    vm = pltpu.VMEM
    scratch = [vm((tm, 3 * d), _F32), vm((hist + tm, d), _F32), vm((tm, d), _F32),
               vm((tm, d), _F32)]
    batch_major_in = x2.ndim == 3
    if batch_major_in:
        body = functools.partial(_sconv_gather_kernel, n_tiles=x2.size // (d * tm),
                                 alpha=alpha, batch=batch, tm=tm, d=d)
        scratch += [vm((3, tm, d), _F32), pltpu.SemaphoreType.DMA((3,))]
    else:
        body = functools.partial(_sconv_kernel, alpha=alpha, batch=batch, tm=tm, d=d)
    return _sublayer_call(
        body, "sconv_mixer_in" if batch_major_in else "sconv_mixer", x2,
        [_layer(p["w_in"], j), _layer(p["conv_w"], j), _layer(p["conv_b"], j),
         _layer(p["w_out"], j)],
        scratch, tm, x_in_hbm=batch_major_in)


def _rglru_layer(x2, p, j, *, alpha, batch):
    tm = ROW_TILE
    d = x2.shape[1]
    r_width = p["w_out"].shape[1]
    heads = p["w_gate"].shape[1]
    blk = r_width // heads
    assert r_width % COL_BLOCK == 0 and COL_BLOCK % blk == 0
    assert (RGLRU_EARLY_OUT_HEADS * blk) % COL_BLOCK == 0 and RGLRU_EARLY_OUT_HEADS < heads
    hist = (p["conv_w"].shape[1] - 1) * batch
    vm = pltpu.VMEM
    body = functools.partial(_rglru_kernel, alpha=alpha, batch=batch, tm=tm,
                             r_width=r_width, heads=heads)
    wide = vm((tm, r_width), _F32)
    return _sublayer_call(
        body, "rglru_mixer", x2,
        [_layer(p["w_in"], j), _layer(p["b_in"], j), _layer(p["conv_w"], j),
         _layer(p["conv_b"], j), _layer(p["w_gate"], j), _layer(p["b_gate"], j),
         _layer(p["lam"], j), _layer(p["w_out"], j)],
        [vm((hist + tm, r_width), _F32), wide, wide, vm((2, tm, 2 * blk), _F32),
         vm((batch, r_width), _F32), wide, vm((tm, d), _F32), vm((tm, d), _F32)],
        tm)


def _ffn_layer(z2, p, i, ln_in, ln, *, alpha, batch, batch_major_out=False):
    tm = ROW_TILE
    d = z2.shape[1]
    f = p["w_down"].shape[1]
    assert f % COL_BLOCK == 0 and FFN_LEAD_PAIRS <= f // COL_BLOCK
    hist = (p["conv_w"].shape[1] - 1) * batch
    vm = pltpu.VMEM
    body = functools.partial(_ffn_kernel, alpha=alpha, batch=batch, tm=tm, f=f,
                             batch_major_out=batch_major_out)
    scratch = [vm((hist + tm, 2 * f), _F32), vm((tm, f), _F32), vm((tm, d), _F32),
               vm((2, tm, LANES), _F32), vm((tm, d), _F32), vm((tm, d), _F32)]
    if batch_major_out:
        scratch += [vm((2, tm, d), _F32), pltpu.SemaphoreType.DMA((2,))]
    return _sublayer_call(
        body, "conv_ffn_out" if batch_major_out else "conv_ffn", z2,
        [_layer(ln_in["g"], *ln_in["at"]), _layer(ln_in["b"], *ln_in["at"]), _layer(p["w_up"], i), _layer(p["conv_w"], i), _layer(p["conv_b"], i),
         _layer(p["w_down"], i), _layer(ln["g"], *ln["at"]), _layer(ln["b"], *ln["at"])],
        scratch, tm, out_batch=batch if batch_major_out else None)


def kernel(x, sc_w_in, sc_conv_w, sc_conv_b, sc_w_out, lru_w_in, lru_b_in, lru_conv_w, lru_conv_b,
           lru_w_gate, lru_b_gate, lru_lambda, lru_w_out, ffn_w_up, ffn_conv_w, ffn_conv_b,
           ffn_w_down, ln_g, ln_b):
    batch, seq, d = x.shape
    depth = ffn_w_up.shape[0]
    alpha = (2.0 * depth) ** 0.25
    assert ROW_TILE % batch == 0 and (seq * batch) % ROW_TILE == 0 and batch % 8 == 0
    assert ROW_TILE % ROW_BLOCK == 0 and d % LANES == 0

    sconv = dict(w_in=sc_w_in.astype(_BF16), conv_w=sc_conv_w, conv_b=_rows(sc_conv_b),
                 w_out=sc_w_out.astype(_BF16))
    rglru = dict(w_in=lru_w_in.astype(_BF16), b_in=_rows(lru_b_in), conv_w=lru_conv_w,
                 conv_b=_rows(lru_conv_b), w_gate=lru_w_gate.astype(_BF16),
                 b_gate=_rows(lru_b_gate), lam=_rows(lru_lambda),
                 w_out=(0.5 * lru_w_out).astype(_BF16))
    ffn = dict(w_up=ffn_w_up.astype(_BF16), conv_w=ffn_conv_w, conv_b=_rows(ffn_conv_b),
               w_down=ffn_w_down.astype(_BF16))
    ln_g, ln_b = _rows(ln_g), _rows(ln_b)

    assert depth >= 1 and x.dtype == _F32
    x2 = x
    for i in range(depth):
        j = i // 2
        if i % 2 == 0:
            z2 = _sconv_layer(x2, sconv, j, alpha=alpha, batch=batch)
        else:
            z2 = _rglru_layer(x2, rglru, j, alpha=alpha, batch=batch)
        ln_mixer = dict(g=ln_g, b=ln_b, at=(i, 0))
        ln_ffn = dict(g=ln_g, b=ln_b, at=(i, 1))
        x2 = _ffn_layer(z2, ffn, i, ln_mixer, ln_ffn, alpha=alpha, batch=batch,
                        batch_major_out=(i == depth - 1))
    return x2
```

```python
import functools
import math

import jax
import jax.numpy as jnp
from jax import lax
from jax.experimental import pallas as pl
from jax.experimental.pallas import tpu as pltpu

LN_EPS = 1e-5
LRU_C = 8.0
ROW_TILE = 512
ROW_BLOCK = 32
COL_BLOCK = 256
LANES = 128
LN_ROWS = 8
FFN_LEAD_PAIRS = 1
SCONV_LEAD_PAIRS = 2
RGLRU_EARLY_OUT_HEADS = 4
RGLRU_TAIL_R_CHUNKS = 0
VMEM_LIMIT_BYTES = 56 * 1024 * 1024
F32_MIN_NORMAL = 1.1754943508222875e-38
_BF16 = jnp.bfloat16
_F32 = jnp.float32


def _twice_gelu_tanh(z):
    c = math.sqrt(2.0 / math.pi)
    return z + z * jnp.tanh(z * (c + (c * 0.044715) * (z * z)))


def _residual_ln_store(x_ref, z_ref, g_ref, b_ref, o_ref, mu_ref, rs_ref, alpha, tm, y2_ref=None):
    d = z_ref.shape[1]
    lanes = mu_ref.shape[1]
    inv_d = 1.0 / d
    nr = LN_ROWS

    def lane_total(r, fn):
        rows = pl.ds(r, nr)
        acc = fn(rows, 0)
        for c in range(lanes, d, lanes):
            acc = acc + fn(rows, c)
        return jnp.broadcast_to(jnp.sum(acc, axis=-1, keepdims=True), (nr, lanes))

    def residual_sum(rows, c):
        cols = slice(c, c + lanes)
        z = alpha * x_ref[rows, cols] + z_ref[rows, cols]
        if y2_ref is not None:
            z = z + y2_ref[rows, cols]
        z_ref[rows, cols] = z
        return z

    for r in range(0, tm, nr):
        mu_ref[pl.ds(r, nr), :] = inv_d * lane_total(r, residual_sum)

    def centred_sq(rows, c):
        cols = slice(c, c + lanes)
        zc = z_ref[rows, cols] - mu_ref[rows, :]
        z_ref[rows, cols] = zc
        return zc * zc

    for r in range(0, tm, nr):
        rs_ref[pl.ds(r, nr), :] = lax.rsqrt(inv_d * lane_total(r, centred_sq) + LN_EPS)

    for r in range(0, tm, nr):
        rows = pl.ds(r, nr)
        rs = rs_ref[rows, :]
        for c in range(0, d, lanes):
            cols = slice(c, c + lanes)
            o_ref[rows, cols] = z_ref[rows, cols] * rs * g_ref[:, cols] + b_ref[:, cols]


def _layer_norm_store(z_ref, g_ref, b_ref, o_ref, tm):
    g = g_ref[...]
    b = b_ref[...]
    for r in range(0, tm, LN_ROWS):
        rows = pl.ds(r, LN_ROWS)
        z = z_ref[rows, :]
        zc = z - jnp.mean(z, axis=-1, keepdims=True)
        var = jnp.mean(zc * zc, axis=-1, keepdims=True)
        o_ref[rows, :] = zc * lax.rsqrt(var + LN_EPS) * g + b


def _residual_store(x_ref, y_ref, o_ref, alpha, tm, y2_ref=None):
    for r in range(0, tm, ROW_BLOCK):
        rows = pl.ds(r, ROW_BLOCK)
        z = alpha * x_ref[rows, :] + y_ref[rows, :]
        if y2_ref is not None:
            z = z + y2_ref[rows, :]
        o_ref[rows, :] = z


def _causal_conv(buf_ref, r, cols, w, bias, hist, batch):
    k_width = w.shape[0]
    acc = bias
    for k in range(k_width):
        off = hist - (k_width - 1 - k) * batch
        acc = acc + buf_ref[pl.ds(r + off, ROW_BLOCK), cols] * w[k:k + 1, :]
    return acc


def _matmul(lhs_f32, rhs_bf16):
    return lax.dot_general(lhs_f32, rhs_bf16, (((1,), (0,)), ((), ())),
                           preferred_element_type=_F32)


def _project(x_ref, w_ref, cols):
    return _matmul(x_ref[...], w_ref[:, cols])


def _ffn_kernel(z_ref, zn_ref, gin_ref, bin_ref, wup_ref, cw_ref, cb_ref, wdn_ref, g_ref, b_ref,
                o_ref, hbuf, act, zbuf, stats, x_ref, xn_ref, *out_scratch,
                alpha, batch, tm, f, batch_major_out):
    hist = (cw_ref.shape[0] - 1) * batch
    fc = COL_BLOCK
    n_pairs = f // fc
    cur = slice(hist, hist + tm)
    step = pl.program_id(0)

    def up_pair(rows_ref, p):
        for c in (p * fc, f + p * fc):
            hbuf[cur, c:c + fc] = _project(rows_ref, wup_ref, slice(c, c + fc))

    @pl.when(pl.program_id(0) == 0)
    def _():
        hbuf[0:hist, :] = jnp.zeros((hist, hbuf.shape[1]), _F32)
        _layer_norm_store(z_ref, gin_ref, bin_ref, xn_ref, tm)
        for p in range(FFN_LEAD_PAIRS):
            up_pair(xn_ref, p)

    x_ref[...] = xn_ref[...]
    for p in range(FFN_LEAD_PAIRS, n_pairs):
        up_pair(x_ref, p)

    for p in range(n_pairs):
        gcols = slice(p * fc, (p + 1) * fc)
        vcols = slice(f + p * fc, f + (p + 1) * fc)
        wg, bg = 0.5 * cw_ref[:, gcols], 0.5 * cb_ref[:, gcols]
        wv, bv = cw_ref[:, vcols], cb_ref[:, vcols]
        for r in range(0, tm, ROW_BLOCK):
            half = _causal_conv(hbuf, r, gcols, wg, bg, hist, batch)
            val = _causal_conv(hbuf, r, vcols, wv, bv, hist, batch)
            act[pl.ds(r, ROW_BLOCK), gcols] = (half + half * jnp.tanh(half)) * val
        for cols in (gcols, vcols):
            hbuf[0:hist, cols] = hbuf[tm:tm + hist, cols]

    zbuf[...] = _matmul(act[...], wdn_ref[...])

    if batch_major_out:
        rows_out, sems = out_scratch
        slot = step % 2
        ln_out = rows_out.at[slot]
    else:
        ln_out = o_ref
    _residual_ln_store(x_ref, zbuf, g_ref, b_ref, ln_out, stats.at[0], stats.at[1], alpha, tm)

    _layer_norm_store(zn_ref, gin_ref, bin_ref, xn_ref, tm)
    for p in range(FFN_LEAD_PAIRS):
        up_pair(xn_ref, p)

    if batch_major_out:
        steps_per_tile = tm // batch

        def slab_copy(of_step, of_slot, t):
            return pltpu.make_async_copy(
                rows_out.at[of_slot, pl.ds(t * batch, batch), :],
                o_ref.at[:, of_step * steps_per_tile + t, :], sems.at[of_slot])

        for t in range(steps_per_tile):
            slab_copy(step, slot, t).start()

        @pl.when(step > 0)
        def _():
            for t in range(steps_per_tile):
                slab_copy(step - 1, 1 - slot, t).wait()

        @pl.when(step == pl.num_programs(0) - 1)
        def _():
            for t in range(steps_per_tile):
                slab_copy(step, slot, t).wait()


def _sconv_kernel(x_ref, xn_ref, win_ref, cw_ref, cb_ref, wout_ref, o_ref,
                  hbuf, cvbuf, gated, zbuf, *, alpha, batch, tm, d):
    hist = (cw_ref.shape[0] - 1) * batch
    dc = COL_BLOCK
    n_chunks = d // dc

    def cv_pair(rows_ref, p):
        for c in (d + p * dc, 2 * d + p * dc):
            hbuf[:, c:c + dc] = _project(rows_ref, win_ref, slice(c, c + dc))

    @pl.when(pl.program_id(0) == 0)
    def _():
        cvbuf[0:hist, :] = jnp.zeros((hist, d), _F32)
        for p in range(SCONV_LEAD_PAIRS):
            cv_pair(x_ref, p)

    for p in range(SCONV_LEAD_PAIRS, n_chunks):
        cv_pair(x_ref, p)
    for p in range(n_chunks):
        hbuf[:, p * dc:(p + 1) * dc] = _project(x_ref, win_ref, slice(p * dc, (p + 1) * dc))

    for p in range(n_chunks):
        c = p * dc
        for r in range(0, tm, ROW_BLOCK):
            rows = pl.ds(r, ROW_BLOCK)
            cvbuf[pl.ds(hist + r, ROW_BLOCK), c:c + dc] = (
                hbuf[rows, d + c:d + c + dc] * hbuf[rows, 2 * d + c:2 * d + c + dc])
    for p in range(n_chunks):
        cols = slice(p * dc, (p + 1) * dc)
        w, bias = cw_ref[:, cols], cb_ref[:, cols]
        for r in range(0, tm, ROW_BLOCK):
            rows = pl.ds(r, ROW_BLOCK)
            u = _causal_conv(cvbuf, r, cols, w, bias, hist, batch)
            gated[rows, cols] = hbuf[rows, cols] * u
        cvbuf[0:hist, cols] = cvbuf[tm:tm + hist, cols]

    zbuf[...] = _matmul(gated[...], wout_ref[...])

    for p in range(SCONV_LEAD_PAIRS):
        cv_pair(xn_ref, p)

    _residual_store(x_ref, zbuf, o_ref, alpha, tm)


def _sconv_gather_kernel(x_hbm, *refs, n_tiles, batch, tm, **static):
    *body_refs, ring, sems = refs
    step = pl.program_id(0)
    steps_per_tile = tm // batch

    def slab_copy(tile, slot, t):
        return pltpu.make_async_copy(
            x_hbm.at[:, tile * steps_per_tile + t, :],
            ring.at[slot, pl.ds(t * batch, batch), :], sems.at[slot])

    def fetch(tile, slot):
        for t in range(steps_per_tile):
            slab_copy(tile, slot, t).start()

    def wait(tile, slot):
        for t in range(steps_per_tile):
            slab_copy(tile, slot, t).wait()

    @pl.when(step == 0)
    def _():
        fetch(0, 0)
        fetch(min(1, n_tiles - 1), 1)
        wait(0, 0)
        wait(min(1, n_tiles - 1), 1)

    ahead_tile = jnp.minimum(step + 2, n_tiles - 1)
    ahead_slot = lax.rem(step + 2, 3)
    fetch(ahead_tile, ahead_slot)

    _sconv_kernel(ring.at[lax.rem(step, 3)], ring.at[lax.rem(step + 1, 3)], *body_refs,
                  batch=batch, tm=tm, **static)

    wait(ahead_tile, ahead_slot)


def _rglru_kernel(x_ref, xn_ref, win_ref, bin_ref, cw_ref, cb_ref, wg_ref, bg_ref, lam_ref,
                  wout_ref, o_ref,
                  rbuf, gbr, xr, gates, state, gated, zbuf, zearly,
                  *, alpha, batch, tm, r_width, heads):
    hist = (cw_ref.shape[0] - 1) * batch
    blk = r_width // heads
    ic = COL_BLOCK
    n_chunks = r_width // ic
    cur = slice(hist, hist + tm)

    w = 0.5 * cw_ref[...]
    w_sum = w[0:1, :]
    for k in range(1, w.shape[0]):
        w_sum = w_sum + w[k:k + 1, :]
    conv_bias = 0.5 * cb_ref[...] + bin_ref[:, r_width:2 * r_width] * w_sum

    def r_chunk(rows_ref, j):
        rbuf[cur, j * ic:(j + 1) * ic] = _project(
            rows_ref, win_ref, slice(r_width + j * ic, r_width + (j + 1) * ic))

    def g_chunk(rows_ref, j):
        gbr[:, j * ic:(j + 1) * ic] = _project(rows_ref, win_ref, slice(j * ic, (j + 1) * ic))

    def conv_chunk(j):
        cols = slice(j * ic, (j + 1) * ic)
        wc, bc = w[:, cols], conv_bias[:, cols]
        for r in range(0, tm, ROW_BLOCK):
            rows = pl.ds(r, ROW_BLOCK)
            xr[rows, cols] = _causal_conv(rbuf, r, cols, wc, bc, hist, batch)
        rbuf[0:hist, cols] = rbuf[tm:tm + hist, cols]

    @pl.when(pl.program_id(0) == 0)
    def _():
        rbuf[0:hist, :] = jnp.broadcast_to(-bin_ref[:, r_width:2 * r_width], (hist, r_width))
        state[...] = jnp.zeros(state.shape, _F32)
        for j in range(n_chunks):
            r_chunk(x_ref, j)
        for j in range(n_chunks):
            conv_chunk(j)

    neg_lam = -lam_ref[...]
    softplus = jnp.maximum(neg_lam, 0.0) + jnp.log1p(jnp.exp(-jnp.abs(neg_lam)))
    half_scale = (-0.5 * LRU_C) * softplus

    k_split = RGLRU_EARLY_OUT_HEADS * blk
    for h in range(heads):
        cols = slice(h * blk, (h + 1) * blk)
        gts = gates.at[h % 2]

        gts[...] = _matmul(xr[:, cols], wg_ref[h])
        if h < n_chunks:
            g_chunk(x_ref, h)
        if h == min(RGLRU_EARLY_OUT_HEADS + 1, heads - 1):
            zearly[...] = _matmul(gated[:, 0:k_split], wout_ref[0:k_split, :])

        hs = half_scale[:, cols]
        half_bg = 0.5 * bg_ref[h]
        g_bias = bin_ref[:, cols]
        hstate = state[:, cols]
        for t in range(tm // batch):
            rows = pl.ds(t * batch, batch)
            half_gates = gts[rows, :] + half_bg
            log_a = hs + hs * jnp.tanh(half_gates[:, 0:blk])
            half_xr = xr[rows, cols]
            gated_x = half_xr + half_xr * jnp.tanh(half_gates[:, blk:2 * blk])
            a = jnp.exp(log_a)
            one_minus_a2 = jnp.tanh(log_a) * (-1.0 - a * a)
            mult = one_minus_a2 * lax.rsqrt(jnp.maximum(one_minus_a2, F32_MIN_NORMAL))
            hstate = a * hstate + mult * gated_x
            gated[rows, cols] = hstate * _twice_gelu_tanh(gbr[rows, cols] + g_bias)
        state[:, cols] = hstate

    for j in range(n_chunks - RGLRU_TAIL_R_CHUNKS):
        r_chunk(xn_ref, j)

    zbuf[...] = _matmul(gated[:, k_split:r_width], wout_ref[k_split:r_width, :])

    for j in range(n_chunks - RGLRU_TAIL_R_CHUNKS, n_chunks):
        r_chunk(xn_ref, j)
    for j in range(n_chunks):
        conv_chunk(j)

    _residual_store(x_ref, zbuf, o_ref, alpha, tm, y2_ref=zearly)


def _layer(stacked, *index):
    return stacked, index


def _resident(operand):
    stacked, index = operand
    block = (None,) * len(index) + stacked.shape[len(index):]
    origin = index + (0,) * (stacked.ndim - len(index))
    return pl.BlockSpec(block, lambda s: origin, pipeline_mode=pl.Buffered(1))


def _sublayer_call(body, name, x2, operands, scratch, tm, out_batch=None, x_in_hbm=False):
    d = x2.shape[-1]
    rows = x2.size // d
    n_tiles = rows // tm
    cur_spec = pl.BlockSpec((tm, d), lambda s: (s, 0))
    next_spec = pl.BlockSpec((tm, d), lambda s: (jnp.minimum(s + 1, n_tiles - 1), 0))
    if x_in_hbm:
        row_specs, row_args = [pl.BlockSpec(memory_space=pl.ANY)], [x2]
    else:
        row_specs, row_args = [cur_spec, next_spec], [x2, x2]
    if out_batch is None:
        out_shape, out_spec = jax.ShapeDtypeStruct((rows, d), _F32), cur_spec
    else:
        out_shape = jax.ShapeDtypeStruct((out_batch, rows // out_batch, d), _F32)
        out_spec = pl.BlockSpec(memory_space=pl.ANY)
    return pl.pallas_call(
        body,
        out_shape=out_shape,
        grid=(n_tiles,),
        in_specs=row_specs + [_resident(op) for op in operands],
        out_specs=out_spec,
        scratch_shapes=scratch,
        compiler_params=pltpu.CompilerParams(
            dimension_semantics=("arbitrary",),
            vmem_limit_bytes=VMEM_LIMIT_BYTES),
        name=name,
    )(*row_args, *[stacked for stacked, _ in operands])


def _rows(v):
    return v.reshape(v.shape[:-1] + (1, v.shape[-1]))


def _sconv_layer(x2, p, j, *, alpha, batch):
    tm = ROW_TILE
    d = x2.shape[-1]
    assert d % COL_BLOCK == 0 and SCONV_LEAD_PAIRS <= d // COL_BLOCK
    hist = (p["conv_w"].shape[1] - 1) * batch
    vm = pltpu.VMEM
    scratch = [vm((tm, 3 * d), _F32), vm((hist + tm, d), _F32), vm((tm, d), _F32),
               vm((tm, d), _F32)]
    batch_major_in = x2.ndim == 3
    if batch_major_in:
        body = functools.partial(_sconv_gather_kernel, n_tiles=x2.size // (d * tm),
                                 alpha=alpha, batch=batch, tm=tm, d=d)
        scratch += [vm((3, tm, d), _F32), pltpu.SemaphoreType.DMA((3,))]
    else:
        body = functools.partial(_sconv_kernel, alpha=alpha, batch=batch, tm=tm, d=d)
    return _sublayer_call(
        body, "sconv_mixer_in" if batch_major_in else "sconv_mixer", x2,
        [_layer(p["w_in"], j), _layer(p["conv_w"], j), _layer(p["conv_b"], j),
         _layer(p["w_out"], j)],
        scratch, tm, x_in_hbm=batch_major_in)


def _rglru_layer(x2, p, j, *, alpha, batch):
    tm = ROW_TILE
    d = x2.shape[1]
    r_width = p["w_out"].shape[1]
    heads = p["w_gate"].shape[1]
    blk = r_width // heads
    assert r_width % COL_BLOCK == 0 and COL_BLOCK % blk == 0
    assert (RGLRU_EARLY_OUT_HEADS * blk) % COL_BLOCK == 0 and RGLRU_EARLY_OUT_HEADS < heads
    hist = (p["conv_w"].shape[1] - 1) * batch
    vm = pltpu.VMEM
    body = functools.partial(_rglru_kernel, alpha=alpha, batch=batch, tm=tm,
                             r_width=r_width, heads=heads)
    wide = vm((tm, r_width), _F32)
    return _sublayer_call(
        body, "rglru_mixer", x2,
        [_layer(p["w_in"], j), _layer(p["b_in"], j), _layer(p["conv_w"], j),
         _layer(p["conv_b"], j), _layer(p["w_gate"], j), _layer(p["b_gate"], j),
         _layer(p["lam"], j), _layer(p["w_out"], j)],
        [vm((hist + tm, r_width), _F32), wide, wide, vm((2, tm, 2 * blk), _F32),
         vm((batch, r_width), _F32), wide, vm((tm, d), _F32), vm((tm, d), _F32)],
        tm)


def _ffn_layer(z2, p, i, ln_in, ln, *, alpha, batch, batch_major_out=False):
    tm = ROW_TILE
    d = z2.shape[1]
    f = p["w_down"].shape[1]
    assert f % COL_BLOCK == 0 and FFN_LEAD_PAIRS <= f // COL_BLOCK
    hist = (p["conv_w"].shape[1] - 1) * batch
    vm = pltpu.VMEM
    body = functools.partial(_ffn_kernel, alpha=alpha, batch=batch, tm=tm, f=f,
                             batch_major_out=batch_major_out)
    scratch = [vm((hist + tm, 2 * f), _F32), vm((tm, f), _F32), vm((tm, d), _F32),
               vm((2, tm, LANES), _F32), vm((tm, d), _F32), vm((tm, d), _F32)]
    if batch_major_out:
        scratch += [vm((2, tm, d), _F32), pltpu.SemaphoreType.DMA((2,))]
    return _sublayer_call(
        body, "conv_ffn_out" if batch_major_out else "conv_ffn", z2,
        [_layer(ln_in["g"], *ln_in["at"]), _layer(ln_in["b"], *ln_in["at"]), _layer(p["w_up"], i), _layer(p["conv_w"], i), _layer(p["conv_b"], i),
         _layer(p["w_down"], i), _layer(ln["g"], *ln["at"]), _layer(ln["b"], *ln["at"])],
        scratch, tm, out_batch=batch if batch_major_out else None)


def kernel(x, sc_w_in, sc_conv_w, sc_conv_b, sc_w_out, lru_w_in, lru_b_in, lru_conv_w, lru_conv_b,
           lru_w_gate, lru_b_gate, lru_lambda, lru_w_out, ffn_w_up, ffn_conv_w, ffn_conv_b,
           ffn_w_down, ln_g, ln_b):
    batch, seq, d = x.shape
    depth = ffn_w_up.shape[0]
    alpha = (2.0 * depth) ** 0.25
    assert ROW_TILE % batch == 0 and (seq * batch) % ROW_TILE == 0 and batch % 8 == 0
    assert ROW_TILE % ROW_BLOCK == 0 and d % LANES == 0

    sconv = dict(w_in=sc_w_in.astype(_BF16), conv_w=sc_conv_w, conv_b=_rows(sc_conv_b),
                 w_out=sc_w_out.astype(_BF16))
    rglru = dict(w_in=lru_w_in.astype(_BF16), b_in=_rows(lru_b_in), conv_w=lru_conv_w,
                 conv_b=_rows(lru_conv_b), w_gate=lru_w_gate.astype(_BF16),
                 b_gate=_rows(lru_b_gate), lam=_rows(lru_lambda),
                 w_out=(0.5 * lru_w_out).astype(_BF16))
    ffn = dict(w_up=ffn_w_up.astype(_BF16), conv_w=ffn_conv_w, conv_b=_rows(ffn_conv_b),
               w_down=ffn_w_down.astype(_BF16))
    ln_g, ln_b = _rows(ln_g), _rows(ln_b)

    assert depth >= 1 and x.dtype == _F32
    x2 = x
    for i in range(depth):
        j = i // 2
        if i % 2 == 0:
            z2 = _sconv_layer(x2, sconv, j, alpha=alpha, batch=batch)
        else:
            z2 = _rglru_layer(x2, rglru, j, alpha=alpha, batch=batch)
        ln_mixer = dict(g=ln_g, b=ln_b, at=(i, 0))
        ln_ffn = dict(g=ln_g, b=ln_b, at=(i, 1))
        x2 = _ffn_layer(z2, ffn, i, ln_mixer, ln_ffn, alpha=alpha, batch=batch,
                        batch_major_out=(i == depth - 1))
    return x2
```

```python
import functools
import math

import jax
import jax.numpy as jnp
from jax import lax
from jax.experimental import pallas as pl
from jax.experimental.pallas import tpu as pltpu

LN_EPS = 1e-5
LRU_C = 8.0
ROW_TILE = 512
ROW_BLOCK = 32
COL_BLOCK = 256
LANES = 128
LN_ROWS = 8
FFN_LEAD_PAIRS = 1
SCONV_LEAD_PAIRS = 2
RGLRU_EARLY_OUT_HEADS = 4
RGLRU_TAIL_R_CHUNKS = 0
VMEM_LIMIT_BYTES = 56 * 1024 * 1024
F32_MIN_NORMAL = 1.1754943508222875e-38
_BF16 = jnp.bfloat16
_F32 = jnp.float32


def _twice_gelu_tanh(z):
    c = math.sqrt(2.0 / math.pi)
    return z + z * jnp.tanh(z * (c + (c * 0.044715) * (z * z)))


def _residual_ln_store(x_ref, z_ref, g_ref, b_ref, o_ref, mu_ref, rs_ref, alpha, tm, y2_ref=None):
    d = z_ref.shape[1]
    lanes = mu_ref.shape[1]
    inv_d = 1.0 / d
    nr = LN_ROWS

    def lane_total(r, fn):
        rows = pl.ds(r, nr)
        acc = fn(rows, 0)
        for c in range(lanes, d, lanes):
            acc = acc + fn(rows, c)
        return jnp.broadcast_to(jnp.sum(acc, axis=-1, keepdims=True), (nr, lanes))

    def residual_sum(rows, c):
        cols = slice(c, c + lanes)
        z = alpha * x_ref[rows, cols] + z_ref[rows, cols]
        if y2_ref is not None:
            z = z + y2_ref[rows, cols]
        z_ref[rows, cols] = z
        return z

    for r in range(0, tm, nr):
        mu_ref[pl.ds(r, nr), :] = inv_d * lane_total(r, residual_sum)

    def centred_sq(rows, c):
        cols = slice(c, c + lanes)
        zc = z_ref[rows, cols] - mu_ref[rows, :]
        z_ref[rows, cols] = zc
        return zc * zc

    for r in range(0, tm, nr):
        rs_ref[pl.ds(r, nr), :] = lax.rsqrt(inv_d * lane_total(r, centred_sq) + LN_EPS)

    for r in range(0, tm, nr):
        rows = pl.ds(r, nr)
        rs = rs_ref[rows, :]
        for c in range(0, d, lanes):
            cols = slice(c, c + lanes)
            o_ref[rows, cols] = z_ref[rows, cols] * rs * g_ref[:, cols] + b_ref[:, cols]


def _layer_norm_store(z_ref, g_ref, b_ref, o_ref, tm):
    g = g_ref[...]
    b = b_ref[...]
    for r in range(0, tm, LN_ROWS):
        rows = pl.ds(r, LN_ROWS)
        z = z_ref[rows, :]
        zc = z - jnp.mean(z, axis=-1, keepdims=True)
        var = jnp.mean(zc * zc, axis=-1, keepdims=True)
        o_ref[rows, :] = zc * lax.rsqrt(var + LN_EPS) * g + b


def _residual_store(x_ref, y_ref, o_ref, alpha, tm, y2_ref=None):
    for r in range(0, tm, ROW_BLOCK):
        rows = pl.ds(r, ROW_BLOCK)
        z = alpha * x_ref[rows, :] + y_ref[rows, :]
        if y2_ref is not None:
            z = z + y2_ref[rows, :]
        o_ref[rows, :] = z


def _causal_conv(buf_ref, r, cols, w, bias, hist, batch):
    k_width = w.shape[0]
    acc = bias
    for k in range(k_width):
        off = hist - (k_width - 1 - k) * batch
        acc = acc + buf_ref[pl.ds(r + off, ROW_BLOCK), cols] * w[k:k + 1, :]
    return acc


def _matmul(lhs_f32, rhs_bf16):
    return lax.dot_general(lhs_f32, rhs_bf16, (((1,), (0,)), ((), ())),
                           preferred_element_type=_F32)


def _project(x_ref, w_ref, cols):
    return _matmul(x_ref[...], w_ref[:, cols])


def _ffn_kernel(z_ref, zn_ref, gin_ref, bin_ref, wup_ref, cw_ref, cb_ref, wdn_ref, g_ref, b_ref,
                o_ref, hbuf, act, zbuf, stats, x_ref, xn_ref, *out_scratch,
                alpha, batch, tm, f, batch_major_out):
    hist = (cw_ref.shape[0] - 1) * batch
    fc = COL_BLOCK
    n_pairs = f // fc
    cur = slice(hist, hist + tm)
    step = pl.program_id(0)

    def up_pair(rows_ref, p):
        for c in (p * fc, f + p * fc):
            hbuf[cur, c:c + fc] = _project(rows_ref, wup_ref, slice(c, c + fc))

    @pl.when(pl.program_id(0) == 0)
    def _():
        hbuf[0:hist, :] = jnp.zeros((hist, hbuf.shape[1]), _F32)
        _layer_norm_store(z_ref, gin_ref, bin_ref, xn_ref, tm)
        for p in range(FFN_LEAD_PAIRS):
            up_pair(xn_ref, p)

    x_ref[...] = xn_ref[...]
    for p in range(FFN_LEAD_PAIRS, n_pairs):
        up_pair(x_ref, p)

    for p in range(n_pairs):
        gcols = slice(p * fc, (p + 1) * fc)
        vcols = slice(f + p * fc, f + (p + 1) * fc)
        wg, bg = 0.5 * cw_ref[:, gcols], 0.5 * cb_ref[:, gcols]
        wv, bv = cw_ref[:, vcols], cb_ref[:, vcols]
        for r in range(0, tm, ROW_BLOCK):
            half = _causal_conv(hbuf, r, gcols, wg, bg, hist, batch)
            val = _causal_conv(hbuf, r, vcols, wv, bv, hist, batch)
            act[pl.ds(r, ROW_BLOCK), gcols] = (half + half * jnp.tanh(half)) * val
        for cols in (gcols, vcols):
            hbuf[0:hist, cols] = hbuf[tm:tm + hist, cols]

    zbuf[...] = _matmul(act[...], wdn_ref[...])

    if batch_major_out:
        rows_out, sems = out_scratch
        slot = step % 2
        ln_out = rows_out.at[slot]
    else:
        ln_out = o_ref
    _residual_ln_store(x_ref, zbuf, g_ref, b_ref, ln_out, stats.at[0], stats.at[1], alpha, tm)

    _layer_norm_store(zn_ref, gin_ref, bin_ref, xn_ref, tm)
    for p in range(FFN_LEAD_PAIRS):
        up_pair(xn_ref, p)

    if batch_major_out:
        steps_per_tile = tm // batch

        def slab_copy(of_step, of_slot, t):
            return pltpu.make_async_copy(
                rows_out.at[of_slot, pl.ds(t * batch, batch), :],
                o_ref.at[:, of_step * steps_per_tile + t, :], sems.at[of_slot])

        for t in range(steps_per_tile):
            slab_copy(step, slot, t).start(priority=t % 2)

        @pl.when(step > 0)
        def _():
            for t in range(steps_per_tile):
                slab_copy(step - 1, 1 - slot, t).wait()

        @pl.when(step == pl.num_programs(0) - 1)
        def _():
            for t in range(steps_per_tile):
                slab_copy(step, slot, t).wait()


def _sconv_kernel(x_ref, xn_ref, win_ref, cw_ref, cb_ref, wout_ref, o_ref,
                  hbuf, cvbuf, gated, zbuf, *, alpha, batch, tm, d):
    hist = (cw_ref.shape[0] - 1) * batch
    dc = COL_BLOCK
    n_chunks = d // dc

    def cv_pair(rows_ref, p):
        for c in (d + p * dc, 2 * d + p * dc):
            hbuf[:, c:c + dc] = _project(rows_ref, win_ref, slice(c, c + dc))

    @pl.when(pl.program_id(0) == 0)
    def _():
        cvbuf[0:hist, :] = jnp.zeros((hist, d), _F32)
        for p in range(SCONV_LEAD_PAIRS):
            cv_pair(x_ref, p)

    for p in range(SCONV_LEAD_PAIRS, n_chunks):
        cv_pair(x_ref, p)
    for p in range(n_chunks):
        hbuf[:, p * dc:(p + 1) * dc] = _project(x_ref, win_ref, slice(p * dc, (p + 1) * dc))

    for p in range(n_chunks):
        c = p * dc
        for r in range(0, tm, ROW_BLOCK):
            rows = pl.ds(r, ROW_BLOCK)
            cvbuf[pl.ds(hist + r, ROW_BLOCK), c:c + dc] = (
                hbuf[rows, d + c:d + c + dc] * hbuf[rows, 2 * d + c:2 * d + c + dc])
    for p in range(n_chunks):
        cols = slice(p * dc, (p + 1) * dc)
        w, bias = cw_ref[:, cols], cb_ref[:, cols]
        for r in range(0, tm, ROW_BLOCK):
            rows = pl.ds(r, ROW_BLOCK)
            u = _causal_conv(cvbuf, r, cols, w, bias, hist, batch)
            gated[rows, cols] = hbuf[rows, cols] * u
        cvbuf[0:hist, cols] = cvbuf[tm:tm + hist, cols]

    zbuf[...] = _matmul(gated[...], wout_ref[...])

    for p in range(SCONV_LEAD_PAIRS):
        cv_pair(xn_ref, p)

    _residual_store(x_ref, zbuf, o_ref, alpha, tm)


def _sconv_gather_kernel(x_hbm, *refs, n_tiles, batch, tm, **static):
    *body_refs, ring, sems = refs
    step = pl.program_id(0)
    steps_per_tile = tm // batch

    def slab_copy(tile, slot, t):
        return pltpu.make_async_copy(
            x_hbm.at[:, tile * steps_per_tile + t, :],
            ring.at[slot, pl.ds(t * batch, batch), :], sems.at[slot])

    def fetch(tile, slot):
        for t in range(steps_per_tile):
            slab_copy(tile, slot, t).start(priority=t % 2)

    def wait(tile, slot):
        for t in range(steps_per_tile):
            slab_copy(tile, slot, t).wait()

    @pl.when(step == 0)
    def _():
        fetch(0, 0)
        fetch(min(1, n_tiles - 1), 1)
        wait(0, 0)
        wait(min(1, n_tiles - 1), 1)

    ahead_tile = jnp.minimum(step + 2, n_tiles - 1)
    ahead_slot = lax.rem(step + 2, 3)
    fetch(ahead_tile, ahead_slot)

    _sconv_kernel(ring.at[lax.rem(step, 3)], ring.at[lax.rem(step + 1, 3)], *body_refs,
                  batch=batch, tm=tm, **static)

    wait(ahead_tile, ahead_slot)


def _rglru_kernel(x_ref, xn_ref, win_ref, bin_ref, cw_ref, cb_ref, wg_ref, bg_ref, lam_ref,
                  wout_ref, o_ref,
                  rbuf, gbr, xr, gates, state, gated, zbuf, zearly,
                  *, alpha, batch, tm, r_width, heads):
    hist = (cw_ref.shape[0] - 1) * batch
    blk = r_width // heads
    ic = COL_BLOCK
    n_chunks = r_width // ic
    cur = slice(hist, hist + tm)

    w = 0.5 * cw_ref[...]
    w_sum = w[0:1, :]
    for k in range(1, w.shape[0]):
        w_sum = w_sum + w[k:k + 1, :]
    conv_bias = 0.5 * cb_ref[...] + bin_ref[:, r_width:2 * r_width] * w_sum

    def r_chunk(rows_ref, j):
        rbuf[cur, j * ic:(j + 1) * ic] = _project(
            rows_ref, win_ref, slice(r_width + j * ic, r_width + (j + 1) * ic))

    def g_chunk(rows_ref, j):
        gbr[:, j * ic:(j + 1) * ic] = _project(rows_ref, win_ref, slice(j * ic, (j + 1) * ic))

    def conv_chunk(j):
        cols = slice(j * ic, (j + 1) * ic)
        wc, bc = w[:, cols], conv_bias[:, cols]
        for r in range(0, tm, ROW_BLOCK):
            rows = pl.ds(r, ROW_BLOCK)
            xr[rows, cols] = _causal_conv(rbuf, r, cols, wc, bc, hist, batch)
        rbuf[0:hist, cols] = rbuf[tm:tm + hist, cols]

    @pl.when(pl.program_id(0) == 0)
    def _():
        rbuf[0:hist, :] = jnp.broadcast_to(-bin_ref[:, r_width:2 * r_width], (hist, r_width))
        state[...] = jnp.zeros(state.shape, _F32)
        for j in range(n_chunks):
            r_chunk(x_ref, j)
        for j in range(n_chunks):
            conv_chunk(j)

    neg_lam = -lam_ref[...]
    softplus = jnp.maximum(neg_lam, 0.0) + jnp.log1p(jnp.exp(-jnp.abs(neg_lam)))
    half_scale = (-0.5 * LRU_C) * softplus

    k_split = RGLRU_EARLY_OUT_HEADS * blk
    for h in range(heads):
        cols = slice(h * blk, (h + 1) * blk)
        gts = gates.at[h % 2]

        gts[...] = _matmul(xr[:, cols], wg_ref[h])
        if h < n_chunks:
            g_chunk(x_ref, h)
        if h == min(RGLRU_EARLY_OUT_HEADS + 1, heads - 1):
            zearly[...] = _matmul(gated[:, 0:k_split], wout_ref[0:k_split, :])

        hs = half_scale[:, cols]
        half_bg = 0.5 * bg_ref[h]
        g_bias = bin_ref[:, cols]
        hstate = state[:, cols]
        for t in range(tm // batch):
            rows = pl.ds(t * batch, batch)
            half_gates = gts[rows, :] + half_bg
            log_a = hs + hs * jnp.tanh(half_gates[:, 0:blk])
            half_xr = xr[rows, cols]
            gated_x = half_xr + half_xr * jnp.tanh(half_gates[:, blk:2 * blk])
            a = jnp.exp(log_a)
            one_minus_a2 = jnp.tanh(log_a) * (-1.0 - a * a)
            mult = one_minus_a2 * lax.rsqrt(jnp.maximum(one_minus_a2, F32_MIN_NORMAL))
            hstate = a * hstate + mult * gated_x
            gated[rows, cols] = hstate * _twice_gelu_tanh(gbr[rows, cols] + g_bias)
        state[:, cols] = hstate

    for j in range(n_chunks - RGLRU_TAIL_R_CHUNKS):
        r_chunk(xn_ref, j)

    zbuf[...] = _matmul(gated[:, k_split:r_width], wout_ref[k_split:r_width, :])

    for j in range(n_chunks - RGLRU_TAIL_R_CHUNKS, n_chunks):
        r_chunk(xn_ref, j)
    for j in range(n_chunks):
        conv_chunk(j)

    _residual_store(x_ref, zbuf, o_ref, alpha, tm, y2_ref=zearly)


def _layer(stacked, *index):
    return stacked, index


def _resident(operand):
    stacked, index = operand
    block = (None,) * len(index) + stacked.shape[len(index):]
    origin = index + (0,) * (stacked.ndim - len(index))
    return pl.BlockSpec(block, lambda s: origin, pipeline_mode=pl.Buffered(1))


def _sublayer_call(body, name, x2, operands, scratch, tm, out_batch=None, x_in_hbm=False):
    d = x2.shape[-1]
    rows = x2.size // d
    n_tiles = rows // tm
    cur_spec = pl.BlockSpec((tm, d), lambda s: (s, 0))
    next_spec = pl.BlockSpec((tm, d), lambda s: (jnp.minimum(s + 1, n_tiles - 1), 0))
    if x_in_hbm:
        row_specs, row_args = [pl.BlockSpec(memory_space=pl.ANY)], [x2]
    else:
        row_specs, row_args = [cur_spec, next_spec], [x2, x2]
    if out_batch is None:
        out_shape, out_spec = jax.ShapeDtypeStruct((rows, d), _F32), cur_spec
    else:
        out_shape = jax.ShapeDtypeStruct((out_batch, rows // out_batch, d), _F32)
        out_spec = pl.BlockSpec(memory_space=pl.ANY)
    return pl.pallas_call(
        body,
        out_shape=out_shape,
        grid=(n_tiles,),
        in_specs=row_specs + [_resident(op) for op in operands],
        out_specs=out_spec,
        scratch_shapes=scratch,
        compiler_params=pltpu.CompilerParams(
            dimension_semantics=("arbitrary",),
            vmem_limit_bytes=VMEM_LIMIT_BYTES),
        name=name,
    )(*row_args, *[stacked for stacked, _ in operands])


def _rows(v):
    return v.reshape(v.shape[:-1] + (1, v.shape[-1]))


def _sconv_layer(x2, p, j, *, alpha, batch):
    tm = ROW_TILE
    d = x2.shape[-1]
    assert d % COL_BLOCK == 0 and SCONV_LEAD_PAIRS <= d // COL_BLOCK
    hist = (p["conv_w"].shape[1] - 1) * batch
    vm = pltpu.VMEM
    scratch = [vm((tm, 3 * d), _F32), vm((hist + tm, d), _F32), vm((tm, d), _F32),
               vm((tm, d), _F32)]
    batch_major_in = x2.ndim == 3
    if batch_major_in:
        body = functools.partial(_sconv_gather_kernel, n_tiles=x2.size // (d * tm),
                                 alpha=alpha, batch=batch, tm=tm, d=d)
        scratch += [vm((3, tm, d), _F32), pltpu.SemaphoreType.DMA((3,))]
    else:
        body = functools.partial(_sconv_kernel, alpha=alpha, batch=batch, tm=tm, d=d)
    return _sublayer_call(
        body, "sconv_mixer_in" if batch_major_in else "sconv_mixer", x2,
        [_layer(p["w_in"], j), _layer(p["conv_w"], j), _layer(p["conv_b"], j),
         _layer(p["w_out"], j)],
        scratch, tm, x_in_hbm=batch_major_in)


def _rglru_layer(x2, p, j, *, alpha, batch):
    tm = ROW_TILE
    d = x2.shape[1]
    r_width = p["w_out"].shape[1]
    heads = p["w_gate"].shape[1]
    blk = r_width // heads
    assert r_width % COL_BLOCK == 0 and COL_BLOCK % blk == 0
    assert (RGLRU_EARLY_OUT_HEADS * blk) % COL_BLOCK == 0 and RGLRU_EARLY_OUT_HEADS < heads
    hist = (p["conv_w"].shape[1] - 1) * batch
    vm = pltpu.VMEM
    body = functools.partial(_rglru_kernel, alpha=alpha, batch=batch, tm=tm,
                             r_width=r_width, heads=heads)
    wide = vm((tm, r_width), _F32)
    return _sublayer_call(
        body, "rglru_mixer", x2,
        [_layer(p["w_in"], j), _layer(p["b_in"], j), _layer(p["conv_w"], j),
         _layer(p["conv_b"], j), _layer(p["w_gate"], j), _layer(p["b_gate"], j),
         _layer(p["lam"], j), _layer(p["w_out"], j)],
        [vm((hist + tm, r_width), _F32), wide, wide, vm((2, tm, 2 * blk), _F32),
         vm((batch, r_width), _F32), wide, vm((tm, d), _F32), vm((tm, d), _F32)],
        tm)


def _ffn_layer(z2, p, i, ln_in, ln, *, alpha, batch, batch_major_out=False):
    tm = ROW_TILE
    d = z2.shape[1]
    f = p["w_down"].shape[1]
    assert f % COL_BLOCK == 0 and FFN_LEAD_PAIRS <= f // COL_BLOCK
    hist = (p["conv_w"].shape[1] - 1) * batch
    vm = pltpu.VMEM
    body = functools.partial(_ffn_kernel, alpha=alpha, batch=batch, tm=tm, f=f,
                             batch_major_out=batch_major_out)
    scratch = [vm((hist + tm, 2 * f), _F32), vm((tm, f), _F32), vm((tm, d), _F32),
               vm((2, tm, LANES), _F32), vm((tm, d), _F32), vm((tm, d), _F32)]
    if batch_major_out:
        scratch += [vm((2, tm, d), _F32), pltpu.SemaphoreType.DMA((2,))]
    return _sublayer_call(
        body, "conv_ffn_out" if batch_major_out else "conv_ffn", z2,
        [_layer(ln_in["g"], *ln_in["at"]), _layer(ln_in["b"], *ln_in["at"]), _layer(p["w_up"], i), _layer(p["conv_w"], i), _layer(p["conv_b"], i),
         _layer(p["w_down"], i), _layer(ln["g"], *ln["at"]), _layer(ln["b"], *ln["at"])],
        scratch, tm, out_batch=batch if batch_major_out else None)


def kernel(x, sc_w_in, sc_conv_w, sc_conv_b, sc_w_out, lru_w_in, lru_b_in, lru_conv_w, lru_conv_b,
           lru_w_gate, lru_b_gate, lru_lambda, lru_w_out, ffn_w_up, ffn_conv_w, ffn_conv_b,
           ffn_w_down, ln_g, ln_b):
    batch, seq, d = x.shape
    depth = ffn_w_up.shape[0]
    alpha = (2.0 * depth) ** 0.25
    assert ROW_TILE % batch == 0 and (seq * batch) % ROW_TILE == 0 and batch % 8 == 0
    assert ROW_TILE % ROW_BLOCK == 0 and d % LANES == 0

    sconv = dict(w_in=sc_w_in.astype(_BF16), conv_w=sc_conv_w, conv_b=_rows(sc_conv_b),
                 w_out=sc_w_out.astype(_BF16))
    rglru = dict(w_in=lru_w_in.astype(_BF16), b_in=_rows(lru_b_in), conv_w=lru_conv_w,
                 conv_b=_rows(lru_conv_b), w_gate=lru_w_gate.astype(_BF16),
                 b_gate=_rows(lru_b_gate), lam=_rows(lru_lambda),
                 w_out=(0.5 * lru_w_out).astype(_BF16))
    ffn = dict(w_up=ffn_w_up.astype(_BF16), conv_w=ffn_conv_w, conv_b=_rows(ffn_conv_b),
               w_down=ffn_w_down.astype(_BF16))
    ln_g, ln_b = _rows(ln_g), _rows(ln_b)

    assert depth >= 1 and x.dtype == _F32
    x2 = x
    for i in range(depth):
        j = i // 2
        if i % 2 == 0:
            z2 = _sconv_layer(x2, sconv, j, alpha=alpha, batch=batch)
        else:
            z2 = _rglru_layer(x2, rglru, j, alpha=alpha, batch=batch)
        ln_mixer = dict(g=ln_g, b=ln_b, at=(i, 0))
        ln_ffn = dict(g=ln_g, b=ln_b, at=(i, 1))
        x2 = _ffn_layer(z2, ffn, i, ln_mixer, ln_ffn, alpha=alpha, batch=batch,
                        batch_major_out=(i == depth - 1))
    return x2
```
